```python
import jax
import jax.numpy as jnp
from jax import lax
import numpy as np

D_MODEL = 2048
BATCH = 2
SEQ = 8192
DEPTH = 2

GRID_W = 64
CTX_LEN = 256
BLOCK = 128
WINDOW = 128
ROPE_BASE = 10000.0
ROPE_DIM = 64
EPS = 1e-6
N_MOD = 9
FFN_DIM = 5632

A_HEADS = 16
A_KV_HEADS = 2
A_HEAD_DIM = 64
A_Q_W = A_HEADS * A_HEAD_DIM
A_KV_W = A_KV_HEADS * A_HEAD_DIM
B_GROUPS = 8
B_GROUP_DIM = 128
B_CHUNK = 128
B_W = B_GROUPS * B_GROUP_DIM
AB_SPLIT = (A_Q_W, A_Q_W + A_KV_W, A_Q_W + 2 * A_KV_W, A_Q_W + 2 * A_KV_W + B_W)
AB_IN = A_Q_W + 2 * A_KV_W + 2 * B_W
AB_OUT = A_Q_W + B_W

C_HEADS = 8
C_NOPE = 128
C_ROPE = ROPE_DIM
C_V = 128
C_Q_RANK = 768
C_KV_RANK = 512
C_W = C_HEADS * C_V
D_W = 1024
D_CONV = 3
CD_SPLIT = (C_Q_RANK, C_Q_RANK + C_KV_RANK, C_Q_RANK + C_KV_RANK + C_ROPE,
            C_Q_RANK + C_KV_RANK + C_ROPE + D_W, C_Q_RANK + C_KV_RANK + C_ROPE + 2 * D_W)
CD_IN = C_Q_RANK + C_KV_RANK + C_ROPE + 3 * D_W
CD_OUT = C_W + D_W

N_AB = (DEPTH + 1) // 2
N_CD = DEPTH // 2

kernel_name = "hybrid_dit_window_gmlp_mla_shortconv"


def rmsnorm(x, g):
    xf = x.astype(jnp.float32)
    y = xf * lax.rsqrt(jnp.mean(xf * xf, axis=-1, keepdims=True) + EPS)
    return (y * g.astype(jnp.float32)).astype(x.dtype)


def modulate(x, g, shift, scale):
    return rmsnorm(x, g) * (1 + scale) + shift


def swiglu(h, w1, w3, w2):
    return (jax.nn.silu(h @ w1) * (h @ w3)) @ w2


def mod_chunks(cond, w, b):
    m = jax.nn.silu(cond) @ w + b
    m = m.reshape(m.shape[0], N_MOD, 1, D_MODEL)
    return [m[:, k] for k in range(N_MOD)]


def axial_angles(n):
    rows = n // GRID_W
    t = jnp.arange(rows * GRID_W)
    row = (t // GRID_W).astype(jnp.float32)
    col = (t % GRID_W).astype(jnp.float32)
    axis_dim = ROPE_DIM // 2
    inv_freq = ROPE_BASE ** (-jnp.arange(0, axis_dim, 2, dtype=jnp.float32) / axis_dim)
    return row[:, None] * inv_freq, col[:, None] * inv_freq


def rope_1d(x, ang):
    xf = x.astype(jnp.float32)
    x1, x2 = jnp.split(xf, 2, axis=-1)
    cos, sin = jnp.cos(ang), jnp.sin(ang)
    return jnp.concatenate([x1 * cos - x2 * sin, x1 * sin + x2 * cos], axis=-1).astype(x.dtype)


def axial_rope(x, ang_r, ang_c):
    xr, xc = jnp.split(x, 2, axis=-1)
    return jnp.concatenate([rope_1d(xr, ang_r), rope_1d(xc, ang_c)], axis=-1)


def window_attention(q, k, v, kc, vc, sink):
    B, n, H, dh = q.shape
    KV = k.shape[2]
    G = H // KV
    nblk = n // BLOCK
    pad = ((0, 0), (BLOCK, BLOCK), (0, 0), (0, 0))
    kp, vp = jnp.pad(k, pad), jnp.pad(v, pad)
    qb = jnp.moveaxis(q.reshape(B, nblk, BLOCK, KV, G, dh), 1, 0)
    sink_kg = sink.reshape(KV, G).astype(jnp.float32)[None, :, :, None, None]
    scale = dh ** -0.5
    offs_q = jnp.arange(BLOCK)
    offs_k = jnp.arange(3 * BLOCK)

    def one_block(args):
        qi, bi = args
        kw = lax.dynamic_slice_in_dim(kp, bi * BLOCK, 3 * BLOCK, axis=1)
        vw = lax.dynamic_slice_in_dim(vp, bi * BLOCK, 3 * BLOCK, axis=1)
        q_pos = bi * BLOCK + offs_q
        k_pos = bi * BLOCK - BLOCK + offs_k
        valid = ((jnp.abs(k_pos[None, :] - q_pos[:, None]) <= WINDOW)
                 & (k_pos >= 0)[None, :] & (k_pos < n)[None, :])
        s_lat = jnp.einsum("bqkgd,bskd->bkgqs", qi, kw).astype(jnp.float32) * scale
        s_lat = jnp.where(valid, s_lat, -jnp.inf)
        s_ctx = jnp.einsum("bqkgd,bskd->bkgqs", qi, kc).astype(jnp.float32) * scale
        s_sink = jnp.broadcast_to(sink_kg, s_ctx.shape[:-1] + (1,))
        p = jax.nn.softmax(jnp.concatenate([s_lat, s_ctx, s_sink], axis=-1), axis=-1).astype(v.dtype)
        o = (jnp.einsum("bkgqs,bskd->bqkgd", p[..., :3 * BLOCK], vw)
             + jnp.einsum("bkgqs,bskd->bqkgd", p[..., 3 * BLOCK:-1], vc))
        return o.reshape(B, BLOCK, H * dh)

    o = lax.map(one_block, (qb, jnp.arange(nblk)))
    return jnp.moveaxis(o, 0, 1).reshape(B, n, H * dh)


def dense_attention_sink(q, k, v, sink):
    B, L, H, dh = q.shape
    KV = k.shape[2]
    G = H // KV
    qg = q.reshape(B, L, KV, G, dh)
    s = jnp.einsum("bqkgd,bskd->bkgqs", qg, k).astype(jnp.float32) * dh ** -0.5
    s_sink = jnp.broadcast_to(sink.reshape(KV, G).astype(jnp.float32)[None, :, :, None, None],
                              s.shape[:-1] + (1,))
    p = jax.nn.softmax(jnp.concatenate([s, s_sink], axis=-1), axis=-1)[..., :-1].astype(v.dtype)
    return jnp.einsum("bkgqs,bskd->bqkgd", p, v).reshape(B, L, H * dh)


def chunk_gmlp(u, v, ws, bias):
    B, T, _ = u.shape
    u = jax.nn.gelu(u)
    v = jax.nn.gelu(v).reshape(B, T // B_CHUNK, B_CHUNK, B_GROUPS, B_GROUP_DIM)
    mixed = jnp.einsum("gpq,bcqgd->bcpgd", ws, v) + bias.T[None, None, :, :, None]
    return u * mixed.reshape(B, T, B_W)


def short_conv(x, w):
    xp = jnp.pad(x, ((0, 0), (1, 1), (0, 0)))
    return w[0] * xp[:, :-2] + w[1] * xp[:, 1:-1] + w[2] * xp[:, 2:]


def mla_queries(cq, g, w_uq):
    B, T, _ = cq.shape
    q = (rmsnorm(cq, g) @ w_uq).reshape(B, T, C_HEADS, C_NOPE + C_ROPE)
    return q[..., :C_NOPE], q[..., C_NOPE:]


def mla_keys_values(ckv, g, w_ukv):
    B, T, _ = ckv.shape
    kv = (rmsnorm(ckv, g) @ w_ukv).reshape(B, T, C_HEADS, C_NOPE + C_V)
    return kv[..., :C_NOPE], kv[..., C_NOPE:]


def mla_attention(q_nope, q_rope, k_nope, k_rope, v):
    B, n, H, _ = q_nope.shape
    nblk = n // BLOCK
    qn = jnp.moveaxis(q_nope.reshape(B, nblk, BLOCK, H, C_NOPE), 1, 0)
    qr = jnp.moveaxis(q_rope.reshape(B, nblk, BLOCK, H, C_ROPE), 1, 0)
    scale = (C_NOPE + C_ROPE) ** -0.5

    def one_block(args):
        qn_i, qr_i = args
        s = jnp.einsum("bqhd,bshd->bhqs", qn_i, k_nope) + jnp.einsum("bqhd,bsd->bhqs", qr_i, k_rope)
        p = jax.nn.softmax(s.astype(jnp.float32) * scale, axis=-1).astype(v.dtype)
        return jnp.einsum("bhqs,bshd->bqhd", p, v)

    o = lax.map(one_block, (qn, qr))
    return jnp.moveaxis(o, 0, 1).reshape(B, n, H * C_V)


def mixer_ab(h, hc, w_in, w_out, sink, ws, bias, ang_r, ang_c, ctx_out):
    B, n, _ = h.shape
    L = hc.shape[1]
    q, k, v, bu, bv = jnp.split(h @ w_in, AB_SPLIT, axis=-1)
    q = axial_rope(q.reshape(B, n, A_HEADS, A_HEAD_DIM), ang_r[:, None], ang_c[:, None])
    k = axial_rope(k.reshape(B, n, A_KV_HEADS, A_HEAD_DIM), ang_r[:, None], ang_c[:, None])
    v = v.reshape(B, n, A_KV_HEADS, A_HEAD_DIM)
    if ctx_out:
        qc, kc, vc, buc, bvc = jnp.split(hc @ w_in, AB_SPLIT, axis=-1)
    else:
        kc, vc = jnp.split(hc @ w_in[:, A_Q_W:A_Q_W + 2 * A_KV_W], 2, axis=-1)
    kc = kc.reshape(B, L, A_KV_HEADS, A_HEAD_DIM)
    vc = vc.reshape(B, L, A_KV_HEADS, A_HEAD_DIM)
    o_a = window_attention(q, k, v, kc, vc, sink)
    o_b = chunk_gmlp(bu, bv, ws, bias)
    y = jnp.concatenate([o_a, o_b], axis=-1) @ w_out
    if not ctx_out:
        return y, None
    oc_a = dense_attention_sink(qc.reshape(B, L, A_HEADS, A_HEAD_DIM), kc, vc, sink)
    oc_b = chunk_gmlp(buc, bvc, ws, bias)
    return y, jnp.concatenate([oc_a, oc_b], axis=-1) @ w_out


def mixer_cd(h, hc, w_in, w_out, q_norm, kv_norm, w_uq, w_ukv, conv_w, ang_r, ang_c, ctx_out):
    cq, ckv, kr, db, dc, dx = jnp.split(h @ w_in, CD_SPLIT, axis=-1)
    q_nope, q_rope = mla_queries(cq, q_norm, w_uq)
    q_rope = axial_rope(q_rope, ang_r[:, None], ang_c[:, None])
    k_nope, v = mla_keys_values(ckv, kv_norm, w_ukv)
    k_rope = axial_rope(kr, ang_r, ang_c)
    if ctx_out:
        cqc, ckvc, krc, dbc, dcc, dxc = jnp.split(hc @ w_in, CD_SPLIT, axis=-1)
    else:
        ckvc, krc = jnp.split(hc @ w_in[:, C_Q_RANK:C_Q_RANK + C_KV_RANK + C_ROPE], (C_KV_RANK,), axis=-1)
    k_nope_c, v_c = mla_keys_values(ckvc, kv_norm, w_ukv)
    o_c = mla_attention(q_nope, q_rope,
                        jnp.concatenate([k_nope, k_nope_c], axis=1),
                        jnp.concatenate([k_rope, krc], axis=1),
                        jnp.concatenate([v, v_c], axis=1))
    o_d = db * short_conv(dc * dx, conv_w)
    y = jnp.concatenate([o_c, o_d], axis=-1) @ w_out
    if not ctx_out:
        return y, None
    qn_c, qr_c = mla_queries(cqc, q_norm, w_uq)
    oc_c = mla_attention(qn_c, qr_c, k_nope_c, krc, v_c)
    oc_d = dbc * short_conv(dcc * dxc, conv_w)
    return y, jnp.concatenate([oc_c, oc_d], axis=-1) @ w_out


def setup_inputs(seed: int = 0) -> dict:
    key = jax.random.key(seed)
    ks = jax.random.split(key, 23)

    def nrm(k, shape, scale):
        return jax.random.normal(k, shape, jnp.float32) * scale

    return {
        "x": nrm(ks[0], (BATCH, SEQ, D_MODEL), 1.0),
        "c": nrm(ks[1], (BATCH, D_MODEL), 1.0),
        "ctx": nrm(ks[2], (BATCH, CTX_LEN, D_MODEL), 1.0),
        "c_ctx": nrm(ks[3], (D_MODEL,), 1.0),
        "mod_w": nrm(ks[4], (DEPTH, D_MODEL, N_MOD * D_MODEL), 0.5 * D_MODEL ** -0.5),
        "mod_b": nrm(ks[5], (DEPTH, N_MOD * D_MODEL), 0.02),
        "norm_g": 1.0 + nrm(ks[6], (DEPTH, 3, D_MODEL), 0.02),
        "ffn_w1": nrm(ks[7], (DEPTH, 2, D_MODEL, FFN_DIM), D_MODEL ** -0.5),
        "ffn_w3": nrm(ks[8], (DEPTH, 2, D_MODEL, FFN_DIM), D_MODEL ** -0.5),
        "ffn_w2": nrm(ks[9], (DEPTH, 2, FFN_DIM, D_MODEL), FFN_DIM ** -0.5),
        "ab_w_in": nrm(ks[10], (N_AB, D_MODEL, AB_IN), D_MODEL ** -0.5),
        "ab_w_out": nrm(ks[11], (N_AB, AB_OUT, D_MODEL), AB_OUT ** -0.5),
        "a_sink": nrm(ks[12], (N_AB, A_HEADS), 0.5),
        "b_ws": nrm(ks[13], (N_AB, B_GROUPS, B_CHUNK, B_CHUNK), B_CHUNK ** -0.5),
        "b_bias": nrm(ks[14], (N_AB, B_GROUPS, B_CHUNK), 0.02),
        "cd_w_in": nrm(ks[15], (N_CD, D_MODEL, CD_IN), D_MODEL ** -0.5),
        "cd_w_out": nrm(ks[16], (N_CD, CD_OUT, D_MODEL), CD_OUT ** -0.5),
        "c_q_norm": 1.0 + nrm(ks[17], (N_CD, C_Q_RANK), 0.02),
        "c_kv_norm": 1.0 + nrm(ks[18], (N_CD, C_KV_RANK), 0.02),
        "c_w_uq": nrm(ks[19], (N_CD, C_Q_RANK, C_HEADS * (C_NOPE + C_ROPE)), C_Q_RANK ** -0.5),
        "c_w_ukv": nrm(ks[20], (N_CD, C_KV_RANK, C_HEADS * (C_NOPE + C_V)), C_KV_RANK ** -0.5),
        "d_conv_w": nrm(ks[21], (N_CD, D_CONV, D_W), D_CONV ** -0.5),
        "final_norm": 1.0 + nrm(ks[22], (D_MODEL,), 0.02),
    }


def reference(x, c, ctx, c_ctx, mod_w, mod_b, norm_g, ffn_w1, ffn_w3, ffn_w2, ab_w_in, ab_w_out,
              a_sink, b_ws, b_bias, cd_w_in, cd_w_out, c_q_norm, c_kv_norm, c_w_uq, c_w_ukv,
              d_conv_w, final_norm):
    ang_r, ang_c = axial_angles(x.shape[1])
    xl, xc = x, ctx
    for i in range(DEPTH):
        last = i == DEPTH - 1
        sh1, sc1, g1, sh2, sc2, g2, sh3, sc3, g3 = mod_chunks(c, mod_w[i], mod_b[i])
        ch1, cs1, cg1, ch2, cs2, cg2, ch3, cs3, cg3 = mod_chunks(c_ctx[None], mod_w[i], mod_b[i])
        xl = xl + 0.5 * g1 * swiglu(modulate(xl, norm_g[i, 0], sh1, sc1), ffn_w1[i, 0], ffn_w3[i, 0], ffn_w2[i, 0])
        xc = xc + 0.5 * cg1 * swiglu(modulate(xc, norm_g[i, 0], ch1, cs1), ffn_w1[i, 0], ffn_w3[i, 0], ffn_w2[i, 0])
        hl = modulate(xl, norm_g[i, 1], sh2, sc2)
        hc = modulate(xc, norm_g[i, 1], ch2, cs2)
        j = i // 2
        if i % 2 == 0:
            yl, yc = mixer_ab(hl, hc, ab_w_in[j], ab_w_out[j], a_sink[j], b_ws[j], b_bias[j],
                              ang_r, ang_c, not last)
        else:
            yl, yc = mixer_cd(hl, hc, cd_w_in[j], cd_w_out[j], c_q_norm[j], c_kv_norm[j], c_w_uq[j],
                              c_w_ukv[j], d_conv_w[j], ang_r, ang_c, not last)
        xl = xl + g2 * yl
        xl = xl + 0.5 * g3 * swiglu(modulate(xl, norm_g[i, 2], sh3, sc3), ffn_w1[i, 1], ffn_w3[i, 1], ffn_w2[i, 1])
        if not last:
            xc = xc + cg2 * yc
            xc = xc + 0.5 * cg3 * swiglu(modulate(xc, norm_g[i, 2], ch3, cs3), ffn_w1[i, 1], ffn_w3[i, 1], ffn_w2[i, 1])
    return rmsnorm(xl, final_norm)
```

```python
import functools

import jax
import jax.numpy as jnp
from jax import lax
from jax.experimental import pallas as pl
from jax.experimental.pallas import tpu as pltpu

F32 = jnp.float32
BF16 = jnp.bfloat16

GRID_W = 64
ROPE_BASE = 10000.0
ROPE_DIM = 64
EPS = 1e-6
N_MOD = 9
WINDOW = 128
A_HEADS = 16
A_KV_HEADS = 2
A_HEAD_DIM = 64
A_Q_W = A_HEADS * A_HEAD_DIM
A_KV_W = A_KV_HEADS * A_HEAD_DIM
B_GROUPS = 8
B_CHUNK = 128
B_W = 1024
C_HEADS = 8
C_NOPE = 128
C_ROPE = ROPE_DIM
C_V = 128
C_Q_RANK = 768
C_KV_RANK = 512
D_W = 1024

LANES = 128
SUBLANES = 8
ROW_TILE = 512
FFN_TILE = 512
ATT_BLOCK = 128
MLA_Q_TILE = 512
MLA_K_TILE = 512
MOD_TILE = 1024
VMEM_LIMIT = 56 * 1024 * 1024


def _cparams(sem):
    return pltpu.CompilerParams(dimension_semantics=sem, vmem_limit_bytes=VMEM_LIMIT)


def _resident(shape):
    nd = len(shape)
    return pl.BlockSpec(shape, lambda *_: (0,) * nd, pipeline_mode=pl.Buffered(1))


def _rms(x):
    return x * lax.rsqrt(jnp.mean(x * x, axis=-1, keepdims=True) + EPS)


def _modulate(x, g, shift, scale):
    return (_rms(x) * g) * (1.0 + scale) + shift


def _rope(z, cos, sin):
    lane = lax.broadcasted_iota(jnp.int32, z.shape, 1)
    first = (lane % 32) < 16
    partner = jnp.where(first, pltpu.roll(z, LANES - 16, 1), pltpu.roll(z, 16, 1))
    return z * cos + partner * sin


def _dot(a, b):
    return jnp.dot(a, b, preferred_element_type=F32)


def _dot_nt(a, b):
    return lax.dot_general(a, b, (((1,), (1,)), ((), ())), preferred_element_type=F32)


def _modvec_kernel(c_ref, w_ref, b_ref, o_ref):
    c = c_ref[...]
    s = c * jax.nn.sigmoid(c)
    o_ref[...] = _dot(s.astype(BF16), w_ref[...].astype(BF16)) + b_ref[...]


def _modvec(cond, mod_w, mod_b):
    depth, d, n = mod_w.shape
    rows = cond.shape[0]
    return pl.pallas_call(
        _modvec_kernel,
        grid=(depth, n // MOD_TILE),
        in_specs=[
            pl.BlockSpec((rows, d), lambda l, j: (0, 0)),
            pl.BlockSpec((None, d, MOD_TILE), lambda l, j: (l, 0, j)),
            pl.BlockSpec((None, 1, MOD_TILE), lambda l, j: (l, 0, j)),
        ],
        out_specs=pl.BlockSpec((None, rows, MOD_TILE), lambda l, j: (l, 0, j)),
        out_shape=jax.ShapeDtypeStruct((depth, rows, n), F32),
        compiler_params=_cparams(("arbitrary", "arbitrary")),
        name="modvec",
    )(cond, mod_w, mod_b.reshape(depth, 1, n))


def _ffn_kernel(*refs, n_ffn_tiles, final):
    if final:
        (x_ref, g_ref, sh_ref, sc_ref, gt_ref, w1_ref, w3_ref, w2_ref, fn_ref,
         o_ref, h_ref, acc_ref) = refs
    else:
        (x_ref, g_ref, sh_ref, sc_ref, gt_ref, w1_ref, w3_ref, w2_ref,
         o_ref, h_ref, acc_ref) = refs
    j = pl.program_id(1)

    @pl.when(j == 0)
    def _():
        h_ref[...] = _modulate(x_ref[...], g_ref[...], sh_ref[...], sc_ref[...]).astype(BF16)
        acc_ref[...] = jnp.zeros_like(acc_ref)

    h = h_ref[...]
    a = _dot(h, w1_ref[...])
    b = _dot(h, w3_ref[...])
    act = ((a * jax.nn.sigmoid(a)) * b).astype(BF16)
    acc_ref[...] += _dot(act, w2_ref[...])

    @pl.when(j == n_ffn_tiles - 1)
    def _():
        y = x_ref[...] + (0.5 * gt_ref[...]) * acc_ref[...]
        if final:
            y = _rms(y) * fn_ref[...]
        o_ref[...] = y


def _ffn(x, n_rows, cidx, g, shift, scale, gate, w1, w3, w2, final_g=None):
    d = x.shape[1]
    f = w1.shape[1]
    nj = f // FFN_TILE
    final = final_g is not None
    vec = pl.BlockSpec((None, 1, d), lambda i, j: (cidx(i), 0, 0))
    in_specs = [
        pl.BlockSpec((ROW_TILE, d), lambda i, j: (i, 0)),
        pl.BlockSpec((1, d), lambda i, j: (0, 0)),
        vec, vec, vec,
        pl.BlockSpec((d, FFN_TILE), lambda i, j: (0, j)),
        pl.BlockSpec((d, FFN_TILE), lambda i, j: (0, j)),
        pl.BlockSpec((FFN_TILE, d), lambda i, j: (j, 0)),
    ]
    args = [x, g.reshape(1, d), shift, scale, gate, w1, w3, w2]
    if final:
        in_specs.append(pl.BlockSpec((1, d), lambda i, j: (0, 0)))
        args.append(final_g.reshape(1, d))
    return pl.pallas_call(
        functools.partial(_ffn_kernel, n_ffn_tiles=nj, final=final),
        grid=(n_rows // ROW_TILE, nj),
        in_specs=in_specs,
        out_specs=pl.BlockSpec((ROW_TILE, d), lambda i, j: (i, 0)),
        out_shape=jax.ShapeDtypeStruct((n_rows, d), F32),
        scratch_shapes=[pltpu.VMEM((ROW_TILE, d), BF16), pltpu.VMEM((ROW_TILE, d), F32)],
        compiler_params=_cparams(("arbitrary", "arbitrary")),
        name="ffn_final" if final else "ffn",
    )(*args)


def _ab_in_kernel(x_ref, g_ref, sh_ref, sc_ref, w_ref, cos_ref, sin_ref,
                  q_ref, kk_ref, vv_ref, u_ref, gv_ref):
    h = _modulate(x_ref[...], g_ref[...], sh_ref[...], sc_ref[...]).astype(BF16)
    cos = cos_ref[...]
    sin = sin_ref[...]
    q_scale = A_HEAD_DIM ** -0.5
    zq = _dot(h, w_ref[:, 0:A_Q_W])
    for p in range(A_Q_W // LANES):
        slab = _rope(zq[:, p * LANES:(p + 1) * LANES], cos, sin)
        q_ref[p] = (slab * q_scale).astype(BF16)
    zkv = _dot(h, w_ref[:, A_Q_W:A_Q_W + 2 * A_KV_W])
    k = _rope(zkv[:, 0:LANES], cos, sin)
    v = zkv[:, LANES:2 * LANES]
    kk_ref[:, 0:LANES] = k.astype(BF16)
    kk_ref[:, LANES:2 * LANES] = pltpu.roll(k, A_HEAD_DIM, 1).astype(BF16)
    vv_ref[:, 0:LANES] = v.astype(BF16)
    vv_ref[:, LANES:2 * LANES] = pltpu.roll(v, A_HEAD_DIM, 1).astype(BF16)
    off = A_Q_W + 2 * A_KV_W
    u_ref[...] = _dot(h, w_ref[:, off:off + B_W])
    gv_ref[...] = jax.nn.gelu(_dot(h, w_ref[:, off + B_W:off + 2 * B_W])).astype(BF16)


def _ab_in(x, cidx, tab_idx, g, shift, scale, w_in, cos_t, sin_t):
    n_rows, d = x.shape
    vec = pl.BlockSpec((None, 1, d), lambda i: (cidx(i), 0, 0))
    tab = pl.BlockSpec((ROW_TILE, LANES), lambda i: (tab_idx(i), 0))
    n_pairs = A_Q_W // LANES
    return pl.pallas_call(
        _ab_in_kernel,
        grid=(n_rows // ROW_TILE,),
        in_specs=[
            pl.BlockSpec((ROW_TILE, d), lambda i: (i, 0)),
            pl.BlockSpec((1, d), lambda i: (0, 0)),
            vec, vec,
            _resident(w_in.shape),
            tab, tab,
        ],
        out_specs=[
            pl.BlockSpec((n_pairs, ROW_TILE, LANES), lambda i: (0, i, 0)),
            pl.BlockSpec((ROW_TILE, 2 * LANES), lambda i: (i, 0)),
            pl.BlockSpec((ROW_TILE, 2 * LANES), lambda i: (i, 0)),
            pl.BlockSpec((ROW_TILE, B_W), lambda i: (i, 0)),
            pl.BlockSpec((ROW_TILE, B_W), lambda i: (i, 0)),
        ],
        out_shape=[
            jax.ShapeDtypeStruct((n_pairs, n_rows, LANES), BF16),
            jax.ShapeDtypeStruct((n_rows, 2 * LANES), BF16),
            jax.ShapeDtypeStruct((n_rows, 2 * LANES), BF16),
            jax.ShapeDtypeStruct((n_rows, B_W), F32),
            jax.ShapeDtypeStruct((n_rows, B_W), BF16),
        ],
        compiler_params=_cparams(("arbitrary",)),
        name="ab_in",
    )(x, g.reshape(1, d), shift, scale, w_in, cos_t, sin_t)


def _win_kernel(sink_ref, q_ref, kp_ref, kc_ref, kn_ref, kx_ref, vp_ref, vc_ref, vn_ref, vx_ref,
                o_ref, *, n_lat_blocks, seq):
    i = pl.program_id(1)
    blk = ATT_BLOCK
    n_lat_keys = 3 * blk
    kall = jnp.concatenate([kp_ref[...], kc_ref[...], kn_ref[...], kx_ref[...]], axis=0)
    vall = jnp.concatenate([vp_ref[...], vc_ref[...], vn_ref[...], vx_ref[...]], axis=0)
    n_keys = kall.shape[0]
    lo = lax.broadcasted_iota(jnp.int32, (n_keys, LANES), 1) < A_HEAD_DIM
    zero = jnp.zeros((n_keys, LANES), BF16)
    k_nat, k_swp = kall[:, 0:LANES], kall[:, LANES:2 * LANES]
    v_nat, v_swp = vall[:, 0:LANES], vall[:, LANES:2 * LANES]

    qo = lax.broadcasted_iota(jnp.int32, (blk, n_keys), 0)
    ko = lax.broadcasted_iota(jnp.int32, (blk, n_keys), 1)
    rel = ko - blk - qo
    kpos = i * blk - blk + ko
    in_window = jnp.where(jnp.abs(rel) <= WINDOW, 1, 0)
    in_range = jnp.where(kpos >= 0, 1, 0) * jnp.where(kpos < seq, 1, 0)
    latent_query = jnp.where(i < n_lat_blocks, 1, 0)
    is_ctx_key = jnp.where(ko >= n_lat_keys, 1, 0)
    valid = (in_window * in_range * latent_query + is_ctx_key) > 0
    bias = jnp.where(valid, 0.0, -1e30).astype(F32)
    pairs_per_kv = (A_HEADS // A_KV_HEADS) // 2
    bias_g = jnp.concatenate([bias] * pairs_per_kv, axis=0)

    for kv in range(A_KV_HEADS):
        qg = q_ref[kv * pairs_per_kv:(kv + 1) * pairs_per_kv].reshape(pairs_per_kv * blk, LANES)
        acc = jnp.zeros((pairs_per_kv * blk, LANES), F32)
        for par in range(2):
            src_k = (k_nat, k_swp) if (kv + par) % 2 == 0 else (k_swp, k_nat)
            src_v = (v_nat, v_swp) if (kv + par) % 2 == 0 else (v_swp, v_nat)
            if par == 0:
                ke = jnp.where(lo, src_k[0], zero)
                ve = jnp.where(lo, src_v[0], zero)
            else:
                ke = jnp.where(lo, zero, src_k[0])
                ve = jnp.where(lo, zero, src_v[0])
            s = _dot_nt(qg, ke) + bias_g
            sink_col = jnp.concatenate(
                [jnp.full((blk, 1), sink_ref[kv * 2 * pairs_per_kv + 2 * pp + par], F32)
                 for pp in range(pairs_per_kv)], axis=0)
            m = jnp.maximum(jnp.max(s, axis=-1, keepdims=True), sink_col)
            p = jnp.exp(s - m)
            denom = jnp.sum(p, axis=-1, keepdims=True) + jnp.exp(sink_col - m)
            acc = acc + _dot(p.astype(BF16), ve) / denom
        for pp in range(pairs_per_kv):
            col = (kv * pairs_per_kv + pp) * LANES
            o_ref[:, col:col + LANES] = acc[pp * blk:(pp + 1) * blk].astype(BF16)


def _win_attention(q, kk, vv, sink, batch, seq, ctx_len):
    n_pairs, n_rows, _ = q.shape
    blk = ATT_BLOCK
    nlb = seq // blk
    ncb = ctx_len // blk
    lat_blocks = batch * nlb

    def q_idx(b, i):
        return jnp.where(i < nlb, b * nlb + i, lat_blocks + b * ncb + (i - nlb))

    def k_idx(b, i, off):
        return b * nlb + jnp.clip(i + off, 0, nlb - 1)

    ctx_block0 = (batch * seq) // ctx_len
    two = 2 * LANES

    def nb(off):
        return pl.BlockSpec((blk, two), lambda b, i, s: (k_idx(b, i, off), 0))

    ctx_spec = pl.BlockSpec((ctx_len, two), lambda b, i, s: (ctx_block0 + b, 0))
    grid_spec = pltpu.PrefetchScalarGridSpec(
        num_scalar_prefetch=1,
        grid=(batch, nlb + ncb),
        in_specs=[
            pl.BlockSpec((n_pairs, blk, LANES), lambda b, i, s: (0, q_idx(b, i), 0)),
            nb(-1), nb(0), nb(1), ctx_spec,
            nb(-1), nb(0), nb(1), ctx_spec,
        ],
        out_specs=pl.BlockSpec((blk, A_Q_W), lambda b, i, s: (q_idx(b, i), 0)),
    )
    return pl.pallas_call(
        functools.partial(_win_kernel, n_lat_blocks=nlb, seq=seq),
        grid_spec=grid_spec,
        out_shape=jax.ShapeDtypeStruct((n_rows, A_Q_W), BF16),
        compiler_params=_cparams(("arbitrary", "arbitrary")),
        name="win_attn",
    )(sink, q, kk, kk, kk, kk, vv, vv, vv, vv)


def _gmlp_kernel(u_ref, gv_ref, ws_ref, bias_ref, o_ref):
    n_chunks = u_ref.shape[0] // B_CHUNK
    for g in range(B_GROUPS):
        cols = slice(g * LANES, (g + 1) * LANES)
        rhs = jnp.concatenate(
            [gv_ref[c * B_CHUNK:(c + 1) * B_CHUNK, cols] for c in range(n_chunks)], axis=1)
        mixed = _dot(ws_ref[g], rhs) + bias_ref[:, g:g + 1]
        for c in range(n_chunks):
            rows = slice(c * B_CHUNK, (c + 1) * B_CHUNK)
            gu = jax.nn.gelu(u_ref[rows, cols])
            o_ref[rows, cols] = (gu * mixed[:, c * LANES:(c + 1) * LANES]).astype(BF16)


def _gmlp(u, gv, ws, bias_t):
    n_rows = u.shape[0]
    return pl.pallas_call(
        _gmlp_kernel,
        grid=(n_rows // ROW_TILE,),
        in_specs=[
            pl.BlockSpec((ROW_TILE, B_W), lambda i: (i, 0)),
            pl.BlockSpec((ROW_TILE, B_W), lambda i: (i, 0)),
            pl.BlockSpec(ws.shape, lambda i: (0, 0, 0)),
            pl.BlockSpec(bias_t.shape, lambda i: (0, 0)),
        ],
        out_specs=pl.BlockSpec((ROW_TILE, B_W), lambda i: (i, 0)),
        out_shape=jax.ShapeDtypeStruct((n_rows, B_W), BF16),
        compiler_params=_cparams(("arbitrary",)),
        name="gmlp",
    )(u, gv, ws, bias_t)


def _ab_out_kernel(x_ref, oa_ref, ob_ref, w_ref, gt_ref, o_ref):
    half = oa_ref.shape[1]
    y = _dot(oa_ref[...], w_ref[0:half, :]) + _dot(ob_ref[...], w_ref[half:2 * half, :])
    o_ref[...] = x_ref[...] + gt_ref[...] * y


def _ab_out(x, oa, ob, w_out, cidx, gate):
    n_rows, d = x.shape
    half = oa.shape[1]
    return pl.pallas_call(
        _ab_out_kernel,
        grid=(n_rows // ROW_TILE,),
        in_specs=[
            pl.BlockSpec((ROW_TILE, d), lambda i: (i, 0)),
            pl.BlockSpec((ROW_TILE, half), lambda i: (i, 0)),
            pl.BlockSpec((ROW_TILE, half), lambda i: (i, 0)),
            _resident(w_out.shape),
            pl.BlockSpec((None, 1, d), lambda i: (cidx(i), 0, 0)),
        ],
        out_specs=pl.BlockSpec((ROW_TILE, d), lambda i: (i, 0)),
        out_shape=jax.ShapeDtypeStruct((n_rows, d), F32),
        compiler_params=_cparams(("arbitrary",)),
        name="ab_out",
    )(x, oa, ob, w_out, gate)


def _cd_in_kernel(x_ref, g_ref, sh_ref, sc_ref, w_ref, qn_ref, kvn_ref, cos_ref, sin_ref,
                  cq_ref, ckv_ref, kr_ref, db_ref, t_ref):
    h = _modulate(x_ref[...], g_ref[...], sh_ref[...], sc_ref[...]).astype(BF16)
    o0 = C_Q_RANK
    o1 = o0 + C_KV_RANK
    o2 = o1 + D_W
    o3 = o2 + D_W
    o4 = o3 + D_W
    cq_ref[...] = (_rms(_dot(h, w_ref[:, 0:o0])) * qn_ref[...]).astype(BF16)
    ckv_ref[...] = (_rms(_dot(h, w_ref[:, o0:o1])) * kvn_ref[...]).astype(BF16)
    db_ref[...] = _dot(h, w_ref[:, o1:o2])
    t_ref[...] = _dot(h, w_ref[:, o2:o3]) * _dot(h, w_ref[:, o3:o4])
    kr_ref[...] = _rope(_dot(h, w_ref[:, o4:o4 + LANES]), cos_ref[...], sin_ref[...]).astype(BF16)


def _cd_in(x, cidx, tab_idx, g, shift, scale, w_in, q_norm, kv_norm, cos_t, sin_t):
    n_rows, d = x.shape
    vec = pl.BlockSpec((None, 1, d), lambda i: (cidx(i), 0, 0))
    tab = pl.BlockSpec((ROW_TILE, LANES), lambda i: (tab_idx(i), 0))

    def rows(width):
        return pl.BlockSpec((ROW_TILE, width), lambda i: (i, 0))

    return pl.pallas_call(
        _cd_in_kernel,
        grid=(n_rows // ROW_TILE,),
        in_specs=[
            rows(d),
            pl.BlockSpec((1, d), lambda i: (0, 0)),
            vec, vec,
            _resident(w_in.shape),
            pl.BlockSpec((1, C_Q_RANK), lambda i: (0, 0)),
            pl.BlockSpec((1, C_KV_RANK), lambda i: (0, 0)),
            tab, tab,
        ],
        out_specs=[rows(C_Q_RANK), rows(C_KV_RANK), rows(LANES), rows(D_W), rows(D_W)],
        out_shape=[
            jax.ShapeDtypeStruct((n_rows, C_Q_RANK), BF16),
            jax.ShapeDtypeStruct((n_rows, C_KV_RANK), BF16),
            jax.ShapeDtypeStruct((n_rows, LANES), BF16),
            jax.ShapeDtypeStruct((n_rows, D_W), F32),
            jax.ShapeDtypeStruct((n_rows, D_W), F32),
        ],
        compiler_params=_cparams(("arbitrary",)),
        name="cd_in",
    )(x, g.reshape(1, d), shift, scale, w_in, q_norm.reshape(1, -1), kv_norm.reshape(1, -1),
      cos_t, sin_t)


def _mla_q_kernel(cq_ref, wn_ref, wr_ref, cos_ref, sin_ref, q_ref):
    cq = cq_ref[...]
    scale = (C_NOPE + C_ROPE) ** -0.5
    qn = _dot(cq, wn_ref[...]) * scale
    qr = _dot(cq, wr_ref[...]) * scale
    cos = cos_ref[...]
    sin = sin_ref[...]
    lo = lax.broadcasted_iota(jnp.int32, (cq.shape[0], LANES), 1) < C_ROPE
    for pair in range(C_HEADS // 2):
        slab = _rope(qr[:, pair * LANES:(pair + 1) * LANES], cos, sin)
        for par in range(2):
            hd = 2 * pair + par
            q_ref[:, 2 * hd * LANES:(2 * hd + 1) * LANES] = (
                qn[:, hd * LANES:(hd + 1) * LANES].astype(BF16))
            rot = jnp.where(lo, slab, 0.0) if par == 0 else jnp.where(lo, 0.0, slab)
            q_ref[:, (2 * hd + 1) * LANES:(2 * hd + 2) * LANES] = rot.astype(BF16)


def _mla_q(cq, n_rows, w_qn, w_qr, cos_t, sin_t, tab_idx):
    width = 2 * LANES * C_HEADS
    tab = pl.BlockSpec((ROW_TILE, LANES), lambda i: (tab_idx(i), 0))
    return pl.pallas_call(
        _mla_q_kernel,
        grid=(n_rows // ROW_TILE,),
        in_specs=[
            pl.BlockSpec((ROW_TILE, C_Q_RANK), lambda i: (i, 0)),
            _resident(w_qn.shape), _resident(w_qr.shape),
            tab, tab,
        ],
        out_specs=pl.BlockSpec((ROW_TILE, width), lambda i: (i, 0)),
        out_shape=jax.ShapeDtypeStruct((n_rows, width), BF16),
        compiler_params=_cparams(("arbitrary",)),
        name="mla_q",
    )(cq, w_qn, w_qr, cos_t, sin_t)


def _mla_kv_kernel(ckv_ref, kr_ref, wk_ref, wv_ref, k_ref, v_ref):
    ckv = ckv_ref[...]
    kn = _dot(ckv, wk_ref[...])
    kr = kr_ref[...]
    for hd in range(C_HEADS):
        k_ref[:, 2 * hd * LANES:(2 * hd + 1) * LANES] = kn[:, hd * LANES:(hd + 1) * LANES].astype(BF16)
        k_ref[:, (2 * hd + 1) * LANES:(2 * hd + 2) * LANES] = kr
    v_ref[...] = _dot(ckv, wv_ref[...]).astype(BF16)


def _mla_kv(ckv, kr2, w_kn, w_v):
    n_rows = ckv.shape[0]
    kw = 2 * LANES * C_HEADS
    vw = C_V * C_HEADS
    return pl.pallas_call(
        _mla_kv_kernel,
        grid=(n_rows // ROW_TILE,),
        in_specs=[
            pl.BlockSpec((ROW_TILE, C_KV_RANK), lambda i: (i, 0)),
            pl.BlockSpec((ROW_TILE, LANES), lambda i: (i, 0)),
            _resident(w_kn.shape), _resident(w_v.shape),
        ],
        out_specs=[pl.BlockSpec((ROW_TILE, kw), lambda i: (i, 0)),
                   pl.BlockSpec((ROW_TILE, vw), lambda i: (i, 0))],
        out_shape=[jax.ShapeDtypeStruct((n_rows, kw), BF16),
                   jax.ShapeDtypeStruct((n_rows, vw), BF16)],
        compiler_params=_cparams(("arbitrary",)),
        name="mla_kv",
    )(ckv, kr2, w_kn, w_v)


def _mla_kernel(q_ref, kl_ref, kx_ref, vl_ref, vx_ref, o_ref, m_ref, l_ref, acc_ref, *, n_k_tiles):
    q = q_ref[...]

    def bcast(col, width):
        return jnp.broadcast_to(col, (col.shape[0], width))

    s = _dot_nt(q, kx_ref[...])
    m0 = jnp.max(s, axis=-1, keepdims=True)
    p = jnp.exp(s - m0)
    m_ref[...] = bcast(m0, LANES)
    l_ref[...] = bcast(jnp.sum(p, axis=-1, keepdims=True), LANES)
    acc_ref[...] = _dot(p.astype(BF16), vx_ref[...])

    def body(c, carry):
        off = pl.multiple_of(c * MLA_K_TILE, MLA_K_TILE)
        k = kl_ref[pl.ds(off, MLA_K_TILE), :]
        v = vl_ref[pl.ds(off, MLA_K_TILE), :]
        s = _dot_nt(q, k)
        m_old = m_ref[...]
        m_new = jnp.maximum(m_old, bcast(jnp.max(s, axis=-1, keepdims=True), LANES))
        alpha = jnp.exp(m_old - m_new)
        p = jnp.exp(s - jnp.concatenate([m_new] * (MLA_K_TILE // LANES), axis=1))
        l_ref[...] = alpha * l_ref[...] + bcast(jnp.sum(p, axis=-1, keepdims=True), LANES)
        acc_ref[...] = alpha * acc_ref[...] + _dot(p.astype(BF16), v)
        m_ref[...] = m_new
        return carry

    lax.fori_loop(0, n_k_tiles, body, 0)
    o_ref[...] = (acc_ref[...] / l_ref[...]).astype(BF16)


def _mla_attention(q, k, v, batch, seq, ctx_len):
    n_lat = batch * seq
    nq = seq // MLA_Q_TILE
    ctx_block0 = n_lat // ctx_len
    kw = 2 * LANES
    return pl.pallas_call(
        functools.partial(_mla_kernel, n_k_tiles=seq // MLA_K_TILE),
        grid=(batch, C_HEADS, nq),
        in_specs=[
            pl.BlockSpec((MLA_Q_TILE, kw), lambda b, h, i: (b * nq + i, h)),
            pl.BlockSpec((seq, kw), lambda b, h, i: (b, h)),
            pl.BlockSpec((ctx_len, kw), lambda b, h, i: (ctx_block0 + b, h)),
            pl.BlockSpec((seq, C_V), lambda b, h, i: (b, h)),
            pl.BlockSpec((ctx_len, C_V), lambda b, h, i: (ctx_block0 + b, h)),
        ],
        out_specs=pl.BlockSpec((MLA_Q_TILE, C_V), lambda b, h, i: (b * nq + i, h)),
        out_shape=jax.ShapeDtypeStruct((n_lat, C_V * C_HEADS), BF16),
        scratch_shapes=[pltpu.VMEM((MLA_Q_TILE, LANES), F32),
                        pltpu.VMEM((MLA_Q_TILE, LANES), F32),
                        pltpu.VMEM((MLA_Q_TILE, C_V), F32)],
        compiler_params=_cparams(("arbitrary", "arbitrary", "arbitrary")),
        name="mla_attn",
    )(q, k, k, v, v)


def _cd_out_kernel(x_ref, oc_ref, db_ref, t_ref, tp_ref, tn_ref, cw_ref, w_ref, gt_ref, o_ref,
                   *, tiles_per_seq):
    i = pl.program_id(0)
    t = t_ref[...]
    rows = t.shape[0]
    row = lax.broadcasted_iota(jnp.int32, t.shape, 0)
    pos = i % tiles_per_seq
    above = jnp.where(pos > 0, tp_ref[SUBLANES - 1:SUBLANES, :], 0.0)
    below = jnp.where(pos < tiles_per_seq - 1, tn_ref[0:1, :], 0.0)
    t_up = jnp.where(row == 0, above, pltpu.roll(t, 1, 0))
    t_dn = jnp.where(row == rows - 1, below, pltpu.roll(t, rows - 1, 0))
    conv = cw_ref[0:1, :] * t_up + cw_ref[1:2, :] * t + cw_ref[2:3, :] * t_dn
    od = (db_ref[...] * conv).astype(BF16)
    half = oc_ref.shape[1]
    y = _dot(oc_ref[...], w_ref[0:half, :]) + _dot(od, w_ref[half:2 * half, :])
    o_ref[...] = x_ref[...] + gt_ref[...] * y


def _cd_out(x, n_rows, oc, db, t, conv_w, w_out, cidx, gate, tiles_per_seq):
    d = x.shape[1]
    halo_per_tile = ROW_TILE // SUBLANES
    n_halo = t.shape[0] // SUBLANES
    return pl.pallas_call(
        functools.partial(_cd_out_kernel, tiles_per_seq=tiles_per_seq),
        grid=(n_rows // ROW_TILE,),
        in_specs=[
            pl.BlockSpec((ROW_TILE, d), lambda i: (i, 0)),
            pl.BlockSpec((ROW_TILE, D_W), lambda i: (i, 0)),
            pl.BlockSpec((ROW_TILE, D_W), lambda i: (i, 0)),
            pl.BlockSpec((ROW_TILE, D_W), lambda i: (i, 0)),
            pl.BlockSpec((SUBLANES, D_W), lambda i: (jnp.maximum(i * halo_per_tile - 1, 0), 0)),
            pl.BlockSpec((SUBLANES, D_W),
                         lambda i: (jnp.minimum((i + 1) * halo_per_tile, n_halo - 1), 0)),
            pl.BlockSpec(conv_w.shape, lambda i: (0, 0)),
            _resident(w_out.shape),
            pl.BlockSpec((None, 1, d), lambda i: (cidx(i), 0, 0)),
        ],
        out_specs=pl.BlockSpec((ROW_TILE, d), lambda i: (i, 0)),
        out_shape=jax.ShapeDtypeStruct((n_rows, d), F32),
        compiler_params=_cparams(("arbitrary",)),
        name="cd_out",
    )(x, oc, db, t, t, t, conv_w, w_out, gate)


def _rope_tables(seq, n_ctx_rows):
    t = jnp.arange(seq)
    row = (t // GRID_W).astype(F32)
    col = (t % GRID_W).astype(F32)
    axis_dim = ROPE_DIM // 2
    inv_freq = ROPE_BASE ** (-jnp.arange(0, axis_dim, 2, dtype=F32) / axis_dim)
    ang_r = row[:, None] * inv_freq
    ang_c = col[:, None] * inv_freq
    cos_h = jnp.concatenate([jnp.cos(ang_r)] * 2 + [jnp.cos(ang_c)] * 2, axis=-1)
    sin_h = jnp.concatenate([-jnp.sin(ang_r), jnp.sin(ang_r), -jnp.sin(ang_c), jnp.sin(ang_c)], axis=-1)
    reps = LANES // ROPE_DIM
    cos_t = jnp.concatenate([jnp.tile(cos_h, (1, reps)), jnp.ones((n_ctx_rows, LANES), F32)], axis=0)
    sin_t = jnp.concatenate([jnp.tile(sin_h, (1, reps)), jnp.zeros((n_ctx_rows, LANES), F32)], axis=0)
    return cos_t, sin_t


def kernel(x, c, ctx, c_ctx, mod_w, mod_b, norm_g, ffn_w1, ffn_w3, ffn_w2, ab_w_in, ab_w_out, a_sink,
           b_ws, b_bias, cd_w_in, cd_w_out, c_q_norm, c_kv_norm, c_w_uq, c_w_ukv, d_conv_w, final_norm):
    batch, seq, d = x.shape
    ctx_len = ctx.shape[1]
    depth = mod_w.shape[0]
    n_lat = batch * seq
    n_ctx = batch * ctx_len
    n_all = n_lat + n_ctx
    assert depth == 2, "layer 0 mixes with A||B, layer 1 (last) with C||D"
    assert seq % ROW_TILE == 0 and n_ctx % ROW_TILE == 0 and ROW_TILE % ctx_len == 0
    assert seq % GRID_W == 0 and seq % MLA_Q_TILE == 0 and seq % MLA_K_TILE == 0
    assert ctx_len % ATT_BLOCK == 0 and n_lat % ctx_len == 0
    assert (N_MOD * d) % MOD_TILE == 0 and ffn_w1.shape[-1] % FFN_TILE == 0 and d % LANES == 0

    tiles_per_seq = seq // ROW_TILE
    n_lat_tiles = n_lat // ROW_TILE

    def cidx(i):
        return jnp.minimum(i // tiles_per_seq, batch)

    def tab_idx(i):
        return jnp.where(i < n_lat_tiles, i % tiles_per_seq, tiles_per_seq + i - n_lat_tiles)

    xs = jnp.concatenate([x.reshape(n_lat, d), ctx.reshape(n_ctx, d)], axis=0)

    cond = jnp.zeros((SUBLANES, d), F32).at[:batch].set(c).at[batch].set(c_ctx)
    mod = _modvec(cond, mod_w, mod_b)
    mod = mod[:, :batch + 1].reshape(depth, batch + 1, N_MOD, 1, d)

    def mvec(layer, k):
        return mod[layer, :, k]

    cos_t, sin_t = _rope_tables(seq, n_ctx)
    w1 = ffn_w1.astype(BF16)
    w3 = ffn_w3.astype(BF16)
    w2 = ffn_w2.astype(BF16)

    xs = _ffn(xs, n_all, cidx, norm_g[0, 0], mvec(0, 0), mvec(0, 1), mvec(0, 2), w1[0, 0], w3[0, 0], w2[0, 0])
    q, kk, vv, u, gv = _ab_in(xs, cidx, tab_idx, norm_g[0, 1], mvec(0, 3), mvec(0, 4),
                              ab_w_in[0].astype(BF16), cos_t, sin_t)
    oa = _win_attention(q, kk, vv, a_sink[0], batch, seq, ctx_len)
    ob = _gmlp(u, gv, b_ws[0].astype(BF16), b_bias[0].T)
    xs = _ab_out(xs, oa, ob, ab_w_out[0].astype(BF16), cidx, mvec(0, 5))
    xs = _ffn(xs, n_all, cidx, norm_g[0, 2], mvec(0, 6), mvec(0, 7), mvec(0, 8), w1[0, 1], w3[0, 1], w2[0, 1])

    xs = _ffn(xs, n_all, cidx, norm_g[1, 0], mvec(1, 0), mvec(1, 1), mvec(1, 2), w1[1, 0], w3[1, 0], w2[1, 0])
    wi = cd_w_in[0]
    o_kr = C_Q_RANK + C_KV_RANK
    w_kr = wi[:, o_kr:o_kr + C_ROPE]
    w_cd = jnp.concatenate([wi[:, :o_kr], wi[:, o_kr + C_ROPE:], w_kr, w_kr], axis=1).astype(BF16)
    cq, ckv, kr2, db, t = _cd_in(xs, cidx, tab_idx, norm_g[1, 1], mvec(1, 3), mvec(1, 4), w_cd,
                                 c_q_norm[0], c_kv_norm[0], cos_t, sin_t)
    w_uq = c_w_uq[0].reshape(C_Q_RANK, C_HEADS, C_NOPE + C_ROPE)
    w_qn = w_uq[:, :, :C_NOPE].reshape(C_Q_RANK, C_HEADS * C_NOPE).astype(BF16)
    w_qr = w_uq[:, :, C_NOPE:].reshape(C_Q_RANK, C_HEADS * C_ROPE).astype(BF16)
    w_ukv = c_w_ukv[0].reshape(C_KV_RANK, C_HEADS, C_NOPE + C_V)
    w_kn = w_ukv[:, :, :C_NOPE].reshape(C_KV_RANK, C_HEADS * C_NOPE).astype(BF16)
    w_v = w_ukv[:, :, C_NOPE:].reshape(C_KV_RANK, C_HEADS * C_V).astype(BF16)
    qm = _mla_q(cq, n_lat, w_qn, w_qr, cos_t, sin_t, tab_idx)
    km, vm = _mla_kv(ckv, kr2, w_kn, w_v)
    oc = _mla_attention(qm, km, vm, batch, seq, ctx_len)
    xl = _cd_out(xs, n_lat, oc, db, t, d_conv_w[0], cd_w_out[0].astype(BF16), cidx, mvec(1, 5), tiles_per_seq)
    out = _ffn(xl, n_lat, cidx, norm_g[1, 2], mvec(1, 6), mvec(1, 7), mvec(1, 8), w1[1, 1], w3[1, 1], w2[1, 1],
               final_g=final_norm)
    return out.reshape(batch, seq, d)
```

```python
import functools

import jax
import jax.numpy as jnp
from jax import lax
from jax.experimental import pallas as pl
from jax.experimental.pallas import tpu as pltpu

F32 = jnp.float32
BF16 = jnp.bfloat16

GRID_W = 64
ROPE_BASE = 10000.0
ROPE_DIM = 64
EPS = 1e-6
LOG2E = 1.4426950408889634
N_MOD = 9
WINDOW = 128
A_HEADS = 16
A_KV_HEADS = 2
A_HEAD_DIM = 64
A_Q_W = A_HEADS * A_HEAD_DIM
A_KV_W = A_KV_HEADS * A_HEAD_DIM
B_GROUPS = 8
B_CHUNK = 128
B_W = 1024
C_HEADS = 8
C_NOPE = 128
C_ROPE = ROPE_DIM
C_V = 128
C_Q_RANK = 768
C_KV_RANK = 512
D_W = 1024

LANES = 128
SUBLANES = 8
ROW_TILE = 512
FFN_TILE = 512
ATT_BLOCK = 128
MLA_Q_TILE = 1024
MLA_K_TILE_MAX = 768
MLA_ROW_BLOCK = 256
MOD_TILE = 1024
VMEM_LIMIT = 56 * 1024 * 1024


def _cparams(sem):
    return pltpu.CompilerParams(dimension_semantics=sem, vmem_limit_bytes=VMEM_LIMIT)


def _resident(shape):
    nd = len(shape)
    return pl.BlockSpec(shape, lambda *_: (0,) * nd, pipeline_mode=pl.Buffered(1))


def _rms(x):
    return x * lax.rsqrt(jnp.mean(x * x, axis=-1, keepdims=True) + EPS)


def _modulate(x, g, shift, scale):
    return (_rms(x) * g) * (1.0 + scale) + shift


def _rope(z, cos, sin):
    lane = lax.broadcasted_iota(jnp.int32, z.shape, 1)
    first = (lane % 32) < 16
    partner = jnp.where(first, pltpu.roll(z, LANES - 16, 1), pltpu.roll(z, 16, 1))
    return z * cos + partner * sin


def _dot(a, b):
    return jnp.dot(a, b, preferred_element_type=F32)


def _dot_nt(a, b):
    return lax.dot_general(a, b, (((1,), (1,)), ((), ())), preferred_element_type=F32)


def _modvec_kernel(c_ref, w_ref, b_ref, o_ref):
    c = c_ref[...]
    s = c * jax.nn.sigmoid(c)
    o_ref[...] = _dot(s.astype(BF16), w_ref[...].astype(BF16)) + b_ref[...]


def _modvec(cond, mod_w, mod_b):
    depth, d, n = mod_w.shape
    rows = cond.shape[0]
    return pl.pallas_call(
        _modvec_kernel,
        grid=(depth, n // MOD_TILE),
        in_specs=[
            pl.BlockSpec((rows, d), lambda l, j: (0, 0)),
            pl.BlockSpec((None, d, MOD_TILE), lambda l, j: (l, 0, j)),
            pl.BlockSpec((None, 1, MOD_TILE), lambda l, j: (l, 0, j)),
        ],
        out_specs=pl.BlockSpec((None, rows, MOD_TILE), lambda l, j: (l, 0, j)),
        out_shape=jax.ShapeDtypeStruct((depth, rows, n), F32),
        compiler_params=_cparams(("arbitrary", "arbitrary")),
        name="modvec",
    )(cond, mod_w, mod_b.reshape(depth, 1, n))


def _ffn_kernel(*refs, n_ffn_tiles, n_pro, n_lat_tiles, final):
    refs = list(refs)
    x0_ref, xn_ref = refs[0:2]
    pos = 2
    c_ref = None
    if n_lat_tiles is not None:
        c_ref = refs[pos]
        pos += 1
    g_ref, sh_ref, sc_ref, shn_ref, scn_ref, gt_ref, w1_ref, w3_ref, w2_ref = refs[pos:pos + 9]
    pos += 9
    fn_ref = None
    if final:
        fn_ref = refs[pos]
        pos += 1
    o_ref, h_ref, res_ref, acc_ref = refs[pos:pos + 4]
    i = pl.program_id(0)
    j = pl.program_id(1)
    cur = i % 2
    g = g_ref[...]

    @pl.when((i == 0) & (j == 0))
    def _():
        x0 = x0_ref[...]
        res_ref[0] = x0
        h_ref[0] = _modulate(x0, g, sh_ref[...], sc_ref[...]).astype(BF16)

    h = h_ref[cur]
    a = _dot(h, w1_ref[...])
    b = _dot(h, w3_ref[...])
    act = ((a * jax.nn.sigmoid(a)) * b).astype(BF16)
    acc_ref[...] = jnp.where(j > 0, acc_ref[...], 0.0) + _dot(act, w2_ref[...])

    pro_rows = xn_ref.shape[0]
    r0 = pl.multiple_of(jnp.minimum(j, n_pro - 1) * pro_rows, pro_rows)
    nxt = xn_ref[...]
    if c_ref is not None:
        nxt = jnp.where(i + 1 < n_lat_tiles, nxt, c_ref[...])
    res_ref[1 - cur, pl.ds(r0, pro_rows), :] = nxt
    h_ref[1 - cur, pl.ds(r0, pro_rows), :] = _modulate(nxt, g, shn_ref[...], scn_ref[...]).astype(BF16)

    @pl.when(j == n_ffn_tiles - 1)
    def _():
        y = res_ref[cur] + (0.5 * gt_ref[...]) * acc_ref[...]
        if final:
            y = _rms(y) * fn_ref[...]
        o_ref[...] = y


def _ffn(x, n_rows, cidx, g, shift, scale, gate, w1, w3, w2, final_g=None, x_ctx=None):
    d = x.shape[1]
    f = w1.shape[1]
    nj = f // FFN_TILE
    n_tiles = n_rows // ROW_TILE
    final = final_g is not None
    n_pro = 1
    while n_pro * 2 <= min(nj, ROW_TILE // 16):
        n_pro *= 2
    pro_rows = ROW_TILE // n_pro
    n_lat_tiles = None
    if x_ctx is not None:
        assert x_ctx.shape[0] == ROW_TILE and x.shape[0] + ROW_TILE == n_rows
        n_lat_tiles = n_tiles - 1
    last_src = (n_lat_tiles if x_ctx is not None else n_tiles) - 1

    def nxt(i):
        return jnp.minimum(i + 1, n_tiles - 1)

    def chunk(j):
        return jnp.minimum(j, n_pro - 1)

    vec = pl.BlockSpec((None, 1, d), lambda i, j: (cidx(i), 0, 0))
    vec_next = pl.BlockSpec((None, 1, d), lambda i, j: (cidx(nxt(i)), 0, 0))
    const_row = pl.BlockSpec((1, d), lambda i, j: (0, 0))
    in_specs = [
        pl.BlockSpec((ROW_TILE, d), lambda i, j: (0, 0), pipeline_mode=pl.Buffered(1)),
        pl.BlockSpec((pro_rows, d), lambda i, j: (jnp.minimum(i + 1, last_src) * n_pro + chunk(j), 0)),
    ]
    args = [x, x]
    if x_ctx is not None:
        in_specs.append(pl.BlockSpec((pro_rows, d), lambda i, j: (chunk(j), 0)))
        args.append(x_ctx)
    in_specs += [
        const_row, vec, vec, vec_next, vec_next, vec,
        pl.BlockSpec((d, FFN_TILE), lambda i, j: (0, j)),
        pl.BlockSpec((d, FFN_TILE), lambda i, j: (0, j)),
        pl.BlockSpec((FFN_TILE, d), lambda i, j: (j, 0)),
    ]
    args += [g.reshape(1, d), shift, scale, shift, scale, gate, w1, w3, w2]
    if final:
        in_specs.append(const_row)
        args.append(final_g.reshape(1, d))
    return pl.pallas_call(
        functools.partial(_ffn_kernel, n_ffn_tiles=nj, n_pro=n_pro, n_lat_tiles=n_lat_tiles, final=final),
        grid=(n_tiles, nj),
        in_specs=in_specs,
        out_specs=pl.BlockSpec((ROW_TILE, d), lambda i, j: (i, 0)),
        out_shape=jax.ShapeDtypeStruct((n_rows, d), F32),
        scratch_shapes=[pltpu.VMEM((2, ROW_TILE, d), BF16), pltpu.VMEM((2, ROW_TILE, d), F32),
                        pltpu.VMEM((ROW_TILE, d), F32)],
        compiler_params=_cparams(("arbitrary", "arbitrary")),
        name="ffn_final" if final else ("ffn_split" if x_ctx is not None else "ffn"),
    )(*args)


def _ab_in_kernel(x_ref, g_ref, sh_ref, sc_ref, w_ref, cos_ref, sin_ref,
                  q_ref, kk_ref, vv_ref, u_ref, gv_ref):
    h = _modulate(x_ref[...], g_ref[...], sh_ref[...], sc_ref[...]).astype(BF16)
    cos = cos_ref[...]
    sin = sin_ref[...]
    q_scale = A_HEAD_DIM ** -0.5
    zq = _dot(h, w_ref[:, 0:A_Q_W])
    for p in range(A_Q_W // LANES):
        slab = _rope(zq[:, p * LANES:(p + 1) * LANES], cos, sin)
        q_ref[p] = (slab * q_scale).astype(BF16)
    zkv = _dot(h, w_ref[:, A_Q_W:A_Q_W + 2 * A_KV_W])
    k = _rope(zkv[:, 0:LANES], cos, sin)
    v = zkv[:, LANES:2 * LANES]
    kk_ref[:, 0:LANES] = k.astype(BF16)
    kk_ref[:, LANES:2 * LANES] = pltpu.roll(k, A_HEAD_DIM, 1).astype(BF16)
    vv_ref[:, 0:LANES] = v.astype(BF16)
    vv_ref[:, LANES:2 * LANES] = pltpu.roll(v, A_HEAD_DIM, 1).astype(BF16)
    off = A_Q_W + 2 * A_KV_W
    u_ref[...] = _dot(h, w_ref[:, off:off + B_W])
    gv_ref[...] = jax.nn.gelu(_dot(h, w_ref[:, off + B_W:off + 2 * B_W])).astype(BF16)


def _ab_in(x, cidx, tab_idx, g, shift, scale, w_in, cos_t, sin_t):
    n_rows, d = x.shape
    vec = pl.BlockSpec((None, 1, d), lambda i: (cidx(i), 0, 0))
    tab = pl.BlockSpec((ROW_TILE, LANES), lambda i: (tab_idx(i), 0))
    n_pairs = A_Q_W // LANES
    return pl.pallas_call(
        _ab_in_kernel,
        grid=(n_rows // ROW_TILE,),
        in_specs=[
            pl.BlockSpec((ROW_TILE, d), lambda i: (i, 0)),
            pl.BlockSpec((1, d), lambda i: (0, 0)),
            vec, vec,
            _resident(w_in.shape),
            tab, tab,
        ],
        out_specs=[
            pl.BlockSpec((n_pairs, ROW_TILE, LANES), lambda i: (0, i, 0)),
            pl.BlockSpec((ROW_TILE, 2 * LANES), lambda i: (i, 0)),
            pl.BlockSpec((ROW_TILE, 2 * LANES), lambda i: (i, 0)),
            pl.BlockSpec((ROW_TILE, B_W), lambda i: (i, 0)),
            pl.BlockSpec((ROW_TILE, B_W), lambda i: (i, 0)),
        ],
        out_shape=[
            jax.ShapeDtypeStruct((n_pairs, n_rows, LANES), BF16),
            jax.ShapeDtypeStruct((n_rows, 2 * LANES), BF16),
            jax.ShapeDtypeStruct((n_rows, 2 * LANES), BF16),
            jax.ShapeDtypeStruct((n_rows, B_W), F32),
            jax.ShapeDtypeStruct((n_rows, B_W), BF16),
        ],
        compiler_params=_cparams(("arbitrary",)),
        name="ab_in",
    )(x, g.reshape(1, d), shift, scale, w_in, cos_t, sin_t)


def _win_kernel(sink_ref, q_ref, kp_ref, kc_ref, kn_ref, kx_ref, vp_ref, vc_ref, vn_ref, vx_ref,
                o_ref, *, n_lat_blocks, seq):
    i = pl.program_id(1)
    blk = ATT_BLOCK
    n_lat_keys = 3 * blk
    kall = jnp.concatenate([kp_ref[...], kc_ref[...], kn_ref[...], kx_ref[...]], axis=0)
    vall = jnp.concatenate([vp_ref[...], vc_ref[...], vn_ref[...], vx_ref[...]], axis=0)
    n_keys = kall.shape[0]
    lo = lax.broadcasted_iota(jnp.int32, (n_keys, LANES), 1) < A_HEAD_DIM
    zero = jnp.zeros((n_keys, LANES), BF16)
    k_nat, k_swp = kall[:, 0:LANES], kall[:, LANES:2 * LANES]
    v_nat, v_swp = vall[:, 0:LANES], vall[:, LANES:2 * LANES]

    qo = lax.broadcasted_iota(jnp.int32, (blk, n_keys), 0)
    ko = lax.broadcasted_iota(jnp.int32, (blk, n_keys), 1)
    rel = ko - blk - qo
    kpos = i * blk - blk + ko
    in_window = jnp.where(jnp.abs(rel) <= WINDOW, 1, 0)
    in_range = jnp.where(kpos >= 0, 1, 0) * jnp.where(kpos < seq, 1, 0)
    latent_query = jnp.where(i < n_lat_blocks, 1, 0)
    is_ctx_key = jnp.where(ko >= n_lat_keys, 1, 0)
    valid = (in_window * in_range * latent_query + is_ctx_key) > 0
    bias = jnp.where(valid, 0.0, -1e30).astype(F32)
    pairs_per_kv = (A_HEADS // A_KV_HEADS) // 2
    bias_g = jnp.concatenate([bias] * pairs_per_kv, axis=0)

    for kv in range(A_KV_HEADS):
        qg = q_ref[kv * pairs_per_kv:(kv + 1) * pairs_per_kv].reshape(pairs_per_kv * blk, LANES)
        acc = jnp.zeros((pairs_per_kv * blk, LANES), F32)
        for par in range(2):
            src_k = (k_nat, k_swp) if (kv + par) % 2 == 0 else (k_swp, k_nat)
            src_v = (v_nat, v_swp) if (kv + par) % 2 == 0 else (v_swp, v_nat)
            if par == 0:
                ke = jnp.where(lo, src_k[0], zero)
                ve = jnp.where(lo, src_v[0], zero)
            else:
                ke = jnp.where(lo, zero, src_k[0])
                ve = jnp.where(lo, zero, src_v[0])
            s = _dot_nt(qg, ke) + bias_g
            sink_col = jnp.concatenate(
                [jnp.full((blk, 1), sink_ref[kv * 2 * pairs_per_kv + 2 * pp + par], F32)
                 for pp in range(pairs_per_kv)], axis=0)
            m = jnp.maximum(jnp.max(s, axis=-1, keepdims=True), sink_col)
            p = jnp.exp(s - m)
            denom = jnp.sum(p, axis=-1, keepdims=True) + jnp.exp(sink_col - m)
            acc = acc + _dot(p.astype(BF16), ve) / denom
        for pp in range(pairs_per_kv):
            col = (kv * pairs_per_kv + pp) * LANES
            o_ref[:, col:col + LANES] = acc[pp * blk:(pp + 1) * blk].astype(BF16)


def _win_attention(q, kk, vv, sink, batch, seq, ctx_len):
    n_pairs, n_rows, _ = q.shape
    blk = ATT_BLOCK
    nlb = seq // blk
    ncb = ctx_len // blk
    lat_blocks = batch * nlb

    def q_idx(b, i):
        return jnp.where(i < nlb, b * nlb + i, lat_blocks + b * ncb + (i - nlb))

    def k_idx(b, i, off):
        return b * nlb + jnp.clip(i + off, 0, nlb - 1)

    ctx_block0 = (batch * seq) // ctx_len
    two = 2 * LANES

    def nb(off):
        return pl.BlockSpec((blk, two), lambda b, i, s: (k_idx(b, i, off), 0))

    ctx_spec = pl.BlockSpec((ctx_len, two), lambda b, i, s: (ctx_block0 + b, 0))
    grid_spec = pltpu.PrefetchScalarGridSpec(
        num_scalar_prefetch=1,
        grid=(batch, nlb + ncb),
        in_specs=[
            pl.BlockSpec((n_pairs, blk, LANES), lambda b, i, s: (0, q_idx(b, i), 0)),
            nb(-1), nb(0), nb(1), ctx_spec,
            nb(-1), nb(0), nb(1), ctx_spec,
        ],
        out_specs=pl.BlockSpec((blk, A_Q_W), lambda b, i, s: (q_idx(b, i), 0)),
    )
    return pl.pallas_call(
        functools.partial(_win_kernel, n_lat_blocks=nlb, seq=seq),
        grid_spec=grid_spec,
        out_shape=jax.ShapeDtypeStruct((n_rows, A_Q_W), BF16),
        compiler_params=_cparams(("arbitrary", "arbitrary")),
        name="win_attn",
    )(sink, q, kk, kk, kk, kk, vv, vv, vv, vv)


def _gmlp_kernel(u_ref, gv_ref, ws_ref, bias_ref, o_ref):
    n_chunks = u_ref.shape[0] // B_CHUNK
    for g in range(B_GROUPS):
        cols = slice(g * LANES, (g + 1) * LANES)
        rhs = jnp.concatenate(
            [gv_ref[c * B_CHUNK:(c + 1) * B_CHUNK, cols] for c in range(n_chunks)], axis=1)
        mixed = _dot(ws_ref[g], rhs) + bias_ref[:, g:g + 1]
        for c in range(n_chunks):
            rows = slice(c * B_CHUNK, (c + 1) * B_CHUNK)
            gu = jax.nn.gelu(u_ref[rows, cols])
            o_ref[rows, cols] = (gu * mixed[:, c * LANES:(c + 1) * LANES]).astype(BF16)


def _gmlp(u, gv, ws, bias_t):
    n_rows = u.shape[0]
    return pl.pallas_call(
        _gmlp_kernel,
        grid=(n_rows // ROW_TILE,),
        in_specs=[
            pl.BlockSpec((ROW_TILE, B_W), lambda i: (i, 0)),
            pl.BlockSpec((ROW_TILE, B_W), lambda i: (i, 0)),
            pl.BlockSpec(ws.shape, lambda i: (0, 0, 0)),
            pl.BlockSpec(bias_t.shape, lambda i: (0, 0)),
        ],
        out_specs=pl.BlockSpec((ROW_TILE, B_W), lambda i: (i, 0)),
        out_shape=jax.ShapeDtypeStruct((n_rows, B_W), BF16),
        compiler_params=_cparams(("arbitrary",)),
        name="gmlp",
    )(u, gv, ws, bias_t)


def _ab_out_kernel(x_ref, oa_ref, ob_ref, w_ref, gt_ref, o_ref):
    half = oa_ref.shape[1]
    y = _dot(oa_ref[...], w_ref[0:half, :]) + _dot(ob_ref[...], w_ref[half:2 * half, :])
    o_ref[...] = x_ref[...] + gt_ref[...] * y


def _ab_out(x, oa, ob, w_out, cidx, gate):
    n_rows, d = x.shape
    half = oa.shape[1]
    return pl.pallas_call(
        _ab_out_kernel,
        grid=(n_rows // ROW_TILE,),
        in_specs=[
            pl.BlockSpec((ROW_TILE, d), lambda i: (i, 0)),
            pl.BlockSpec((ROW_TILE, half), lambda i: (i, 0)),
            pl.BlockSpec((ROW_TILE, half), lambda i: (i, 0)),
            _resident(w_out.shape),
            pl.BlockSpec((None, 1, d), lambda i: (cidx(i), 0, 0)),
        ],
        out_specs=pl.BlockSpec((ROW_TILE, d), lambda i: (i, 0)),
        out_shape=jax.ShapeDtypeStruct((n_rows, d), F32),
        compiler_params=_cparams(("arbitrary",)),
        name="ab_out",
    )(x, oa, ob, w_out, gate)


def _cd_in_kernel(x_ref, g_ref, sh_ref, sc_ref, w_ref, qn_ref, kvn_ref, cos_ref, sin_ref,
                  cq_ref, ckv_ref, kr_ref, db_ref, t_ref):
    h = _modulate(x_ref[...], g_ref[...], sh_ref[...], sc_ref[...]).astype(BF16)
    o0 = C_Q_RANK
    o1 = o0 + C_KV_RANK
    o2 = o1 + D_W
    o3 = o2 + D_W
    o4 = o3 + D_W
    cq_ref[...] = (_rms(_dot(h, w_ref[:, 0:o0])) * qn_ref[...]).astype(BF16)
    ckv_ref[...] = (_rms(_dot(h, w_ref[:, o0:o1])) * kvn_ref[...]).astype(BF16)
    db_ref[...] = _dot(h, w_ref[:, o1:o2])
    t_ref[...] = _dot(h, w_ref[:, o2:o3]) * _dot(h, w_ref[:, o3:o4])
    kr_ref[...] = _rope(_dot(h, w_ref[:, o4:o4 + LANES]), cos_ref[...], sin_ref[...]).astype(BF16)


def _cd_in(x, cidx, tab_idx, g, shift, scale, w_in, q_norm, kv_norm, cos_t, sin_t):
    n_rows, d = x.shape
    vec = pl.BlockSpec((None, 1, d), lambda i: (cidx(i), 0, 0))
    tab = pl.BlockSpec((ROW_TILE, LANES), lambda i: (tab_idx(i), 0))

    def rows(width):
        return pl.BlockSpec((ROW_TILE, width), lambda i: (i, 0))

    return pl.pallas_call(
        _cd_in_kernel,
        grid=(n_rows // ROW_TILE,),
        in_specs=[
            rows(d),
            pl.BlockSpec((1, d), lambda i: (0, 0)),
            vec, vec,
            _resident(w_in.shape),
            pl.BlockSpec((1, C_Q_RANK), lambda i: (0, 0)),
            pl.BlockSpec((1, C_KV_RANK), lambda i: (0, 0)),
            tab, tab,
        ],
        out_specs=[rows(C_Q_RANK), rows(C_KV_RANK), rows(LANES), rows(D_W), rows(D_W)],
        out_shape=[
            jax.ShapeDtypeStruct((n_rows, C_Q_RANK), BF16),
            jax.ShapeDtypeStruct((n_rows, C_KV_RANK), BF16),
            jax.ShapeDtypeStruct((n_rows, LANES), BF16),
            jax.ShapeDtypeStruct((n_rows, D_W), F32),
            jax.ShapeDtypeStruct((n_rows, D_W), F32),
        ],
        compiler_params=_cparams(("arbitrary",)),
        name="cd_in",
    )(x, g.reshape(1, d), shift, scale, w_in, q_norm.reshape(1, -1), kv_norm.reshape(1, -1),
      cos_t, sin_t)


def _mla_q_kernel(cq_ref, wn_ref, wr_ref, cos_ref, sin_ref, q_ref):
    cq = cq_ref[...]
    scale = (C_NOPE + C_ROPE) ** -0.5 * LOG2E
    qn = _dot(cq, wn_ref[...]) * scale
    qr = _dot(cq, wr_ref[...]) * scale
    cos = cos_ref[...]
    sin = sin_ref[...]
    lo = lax.broadcasted_iota(jnp.int32, (cq.shape[0], LANES), 1) < C_ROPE
    for pair in range(C_HEADS // 2):
        slab = _rope(qr[:, pair * LANES:(pair + 1) * LANES], cos, sin)
        for par in range(2):
            hd = 2 * pair + par
            q_ref[:, 2 * hd * LANES:(2 * hd + 1) * LANES] = (
                qn[:, hd * LANES:(hd + 1) * LANES].astype(BF16))
            rot = jnp.where(lo, slab, 0.0) if par == 0 else jnp.where(lo, 0.0, slab)
            q_ref[:, (2 * hd + 1) * LANES:(2 * hd + 2) * LANES] = rot.astype(BF16)


def _mla_q(cq, n_rows, w_qn, w_qr, cos_t, sin_t, tab_idx):
    width = 2 * LANES * C_HEADS
    tab = pl.BlockSpec((ROW_TILE, LANES), lambda i: (tab_idx(i), 0))
    return pl.pallas_call(
        _mla_q_kernel,
        grid=(n_rows // ROW_TILE,),
        in_specs=[
            pl.BlockSpec((ROW_TILE, C_Q_RANK), lambda i: (i, 0)),
            _resident(w_qn.shape), _resident(w_qr.shape),
            tab, tab,
        ],
        out_specs=pl.BlockSpec((ROW_TILE, width), lambda i: (i, 0)),
        out_shape=jax.ShapeDtypeStruct((n_rows, width), BF16),
        compiler_params=_cparams(("arbitrary",)),
        name="mla_q",
    )(cq, w_qn, w_qr, cos_t, sin_t)


def _mla_kv_kernel(ckv_ref, kr_ref, wk_ref, wv_ref, k_ref, v_ref):
    ckv = ckv_ref[...]
    kn = _dot(ckv, wk_ref[...])
    kr = kr_ref[...]
    for hd in range(C_HEADS):
        k_ref[:, 2 * hd * LANES:(2 * hd + 1) * LANES] = kn[:, hd * LANES:(hd + 1) * LANES].astype(BF16)
        k_ref[:, (2 * hd + 1) * LANES:(2 * hd + 2) * LANES] = kr
    v_ref[...] = _dot(ckv, wv_ref[...]).astype(BF16)


def _mla_kv(ckv, kr2, w_kn, w_v, batch, seq, ctx_len):
    n_rows = ckv.shape[0]
    tile = ctx_len
    tiles_per_seq = seq // tile
    n_lat_tiles = batch * tiles_per_seq
    kw = 2 * LANES * C_HEADS
    vw = C_V * C_HEADS

    def out_idx(j):
        is_lat = j < n_lat_tiles
        b = jnp.where(is_lat, j // tiles_per_seq, j - n_lat_tiles)
        r = jnp.where(is_lat, j % tiles_per_seq, tiles_per_seq)
        return b, r, 0

    return pl.pallas_call(
        _mla_kv_kernel,
        grid=(n_rows // tile,),
        in_specs=[
            pl.BlockSpec((tile, C_KV_RANK), lambda j: (j, 0)),
            pl.BlockSpec((tile, LANES), lambda j: (j, 0)),
            _resident(w_kn.shape), _resident(w_v.shape),
        ],
        out_specs=[pl.BlockSpec((None, tile, kw), out_idx),
                   pl.BlockSpec((None, tile, vw), out_idx)],
        out_shape=[jax.ShapeDtypeStruct((batch, seq + ctx_len, kw), BF16),
                   jax.ShapeDtypeStruct((batch, seq + ctx_len, vw), BF16)],
        compiler_params=_cparams(("arbitrary",)),
        name="mla_kv",
    )(ckv, kr2, w_kn, w_v)


def _mla_kernel(q_ref, k_ref, v_ref, o_ref, s_ref, m_ref, l_ref, acc_ref, *, tk, n_chunks):
    q = q_ref[...]
    reps = tk // LANES

    def rows(c):
        start = c * tk
        if not isinstance(start, int):
            start = pl.multiple_of(start, tk)
        return pl.ds(start, tk)

    def scores(c, slot):
        s_ref[slot] = _dot_nt(q, k_ref[rows(c), :])

    def absorb(c, slot):
        v = v_ref[rows(c), :]
        for rb in range(q.shape[0] // MLA_ROW_BLOCK):
            blk = slice(rb * MLA_ROW_BLOCK, (rb + 1) * MLA_ROW_BLOCK)
            s = s_ref[slot, blk, :]
            m_old = m_ref[blk, :]
            m_new = jnp.maximum(m_old, jnp.broadcast_to(jnp.max(s, axis=-1, keepdims=True), m_old.shape))
            alpha = jnp.exp2(m_old - m_new)
            p = jnp.exp2(s - jnp.concatenate([m_new] * reps, axis=1))
            l_ref[blk, :] = alpha * l_ref[blk, :] + jnp.broadcast_to(
                jnp.sum(p, axis=-1, keepdims=True), m_old.shape)
            acc_ref[blk, :] = alpha * acc_ref[blk, :] + _dot(p.astype(BF16), v)
            m_ref[blk, :] = m_new

    m_ref[...] = jnp.full(m_ref.shape, -1e30, F32)
    l_ref[...] = jnp.zeros(l_ref.shape, F32)
    acc_ref[...] = jnp.zeros(acc_ref.shape, F32)
    scores(0, 0)
    n_pairs = (n_chunks - 1) // 2

    def pair(i, carry):
        scores(2 * i + 1, 1)
        absorb(2 * i, 0)
        scores(2 * i + 2, 0)
        absorb(2 * i + 1, 1)
        return carry

    lax.fori_loop(0, n_pairs, pair, 0)
    done = 2 * n_pairs
    if n_chunks - done == 2:
        scores(done + 1, 1)
        absorb(done, 0)
        absorb(done + 1, 1)
    else:
        absorb(done, 0)
    o_ref[...] = (acc_ref[...] / l_ref[...]).astype(BF16)


def _mla_key_tile(n_keys):
    best = LANES
    for t in range(LANES, MLA_K_TILE_MAX + 1, LANES):
        if n_keys % t == 0:
            best = t
    return best


def _mla_attention(q, k, v, batch, seq):
    n_lat = batch * seq
    n_keys = k.shape[1]
    tq = MLA_Q_TILE
    nq = seq // tq
    tk = _mla_key_tile(n_keys)
    kw = 2 * LANES
    return pl.pallas_call(
        functools.partial(_mla_kernel, tk=tk, n_chunks=n_keys // tk),
        grid=(batch, C_HEADS, nq),
        in_specs=[
            pl.BlockSpec((tq, kw), lambda b, h, i: (b * nq + i, h)),
            pl.BlockSpec((None, n_keys, kw), lambda b, h, i: (b, 0, h)),
            pl.BlockSpec((None, n_keys, C_V), lambda b, h, i: (b, 0, h)),
        ],
        out_specs=pl.BlockSpec((tq, C_V), lambda b, h, i: (b * nq + i, h)),
        out_shape=jax.ShapeDtypeStruct((n_lat, C_V * C_HEADS), BF16),
        scratch_shapes=[pltpu.VMEM((2, tq, tk), F32),
                        pltpu.VMEM((tq, LANES), F32),
                        pltpu.VMEM((tq, LANES), F32),
                        pltpu.VMEM((tq, C_V), F32)],
        compiler_params=_cparams(("arbitrary", "arbitrary", "arbitrary")),
        name="mla_attn",
    )(q, k, v)


def _cd_out_kernel(x_ref, oc_ref, db_ref, t_ref, tp_ref, tn_ref, cw_ref, w_ref, gt_ref, o_ref,
                   *, tiles_per_seq):
    i = pl.program_id(0)
    t = t_ref[...]
    rows = t.shape[0]
    row = lax.broadcasted_iota(jnp.int32, t.shape, 0)
    pos = i % tiles_per_seq
    above = jnp.where(pos > 0, tp_ref[SUBLANES - 1:SUBLANES, :], 0.0)
    below = jnp.where(pos < tiles_per_seq - 1, tn_ref[0:1, :], 0.0)
    t_up = jnp.where(row == 0, above, pltpu.roll(t, 1, 0))
    t_dn = jnp.where(row == rows - 1, below, pltpu.roll(t, rows - 1, 0))
    conv = cw_ref[0:1, :] * t_up + cw_ref[1:2, :] * t + cw_ref[2:3, :] * t_dn
    od = (db_ref[...] * conv).astype(BF16)
    half = oc_ref.shape[1]
    y = _dot(oc_ref[...], w_ref[0:half, :]) + _dot(od, w_ref[half:2 * half, :])
    o_ref[...] = x_ref[...] + gt_ref[...] * y


def _cd_out(x, n_rows, oc, db, t, conv_w, w_out, cidx, gate, tiles_per_seq):
    d = x.shape[1]
    halo_per_tile = ROW_TILE // SUBLANES
    n_halo = t.shape[0] // SUBLANES
    return pl.pallas_call(
        functools.partial(_cd_out_kernel, tiles_per_seq=tiles_per_seq),
        grid=(n_rows // ROW_TILE,),
        in_specs=[
            pl.BlockSpec((ROW_TILE, d), lambda i: (i, 0)),
            pl.BlockSpec((ROW_TILE, D_W), lambda i: (i, 0)),
            pl.BlockSpec((ROW_TILE, D_W), lambda i: (i, 0)),
            pl.BlockSpec((ROW_TILE, D_W), lambda i: (i, 0)),
            pl.BlockSpec((SUBLANES, D_W), lambda i: (jnp.maximum(i * halo_per_tile - 1, 0), 0)),
            pl.BlockSpec((SUBLANES, D_W),
                         lambda i: (jnp.minimum((i + 1) * halo_per_tile, n_halo - 1), 0)),
            pl.BlockSpec(conv_w.shape, lambda i: (0, 0)),
            _resident(w_out.shape),
            pl.BlockSpec((None, 1, d), lambda i: (cidx(i), 0, 0)),
        ],
        out_specs=pl.BlockSpec((ROW_TILE, d), lambda i: (i, 0)),
        out_shape=jax.ShapeDtypeStruct((n_rows, d), F32),
        compiler_params=_cparams(("arbitrary",)),
        name="cd_out",
    )(x, oc, db, t, t, t, conv_w, w_out, gate)


def _rope_tables(seq, n_ctx_rows):
    t = jnp.arange(seq)
    row = (t // GRID_W).astype(F32)
    col = (t % GRID_W).astype(F32)
    axis_dim = ROPE_DIM // 2
    inv_freq = ROPE_BASE ** (-jnp.arange(0, axis_dim, 2, dtype=F32) / axis_dim)
    ang_r = row[:, None] * inv_freq
    ang_c = col[:, None] * inv_freq
    cos_h = jnp.concatenate([jnp.cos(ang_r)] * 2 + [jnp.cos(ang_c)] * 2, axis=-1)
    sin_h = jnp.concatenate([-jnp.sin(ang_r), jnp.sin(ang_r), -jnp.sin(ang_c), jnp.sin(ang_c)], axis=-1)
    reps = LANES // ROPE_DIM
    cos_t = jnp.concatenate([jnp.tile(cos_h, (1, reps)), jnp.ones((n_ctx_rows, LANES), F32)], axis=0)
    sin_t = jnp.concatenate([jnp.tile(sin_h, (1, reps)), jnp.zeros((n_ctx_rows, LANES), F32)], axis=0)
    return cos_t, sin_t


def kernel(x, c, ctx, c_ctx, mod_w, mod_b, norm_g, ffn_w1, ffn_w3, ffn_w2, ab_w_in, ab_w_out, a_sink,
           b_ws, b_bias, cd_w_in, cd_w_out, c_q_norm, c_kv_norm, c_w_uq, c_w_ukv, d_conv_w, final_norm):
    batch, seq, d = x.shape
    ctx_len = ctx.shape[1]
    depth = mod_w.shape[0]
    n_lat = batch * seq
    n_ctx = batch * ctx_len
    n_all = n_lat + n_ctx
    assert depth == 2, "layer 0 mixes with A||B, layer 1 (last) with C||D"
    assert seq % ROW_TILE == 0 and n_ctx % ROW_TILE == 0 and ROW_TILE % ctx_len == 0
    assert seq % GRID_W == 0 and seq % MLA_Q_TILE == 0 and seq % ctx_len == 0
    assert ctx_len % ATT_BLOCK == 0 and n_lat % ctx_len == 0
    assert (N_MOD * d) % MOD_TILE == 0 and ffn_w1.shape[-1] % FFN_TILE == 0 and d % LANES == 0

    tiles_per_seq = seq // ROW_TILE
    n_lat_tiles = n_lat // ROW_TILE

    def cidx(i):
        return jnp.minimum(i // tiles_per_seq, batch)

    def tab_idx(i):
        return jnp.where(i < n_lat_tiles, i % tiles_per_seq, tiles_per_seq + i - n_lat_tiles)

    cond = jnp.zeros((SUBLANES, d), F32).at[:batch].set(c).at[batch].set(c_ctx)
    mod = _modvec(cond, mod_w, mod_b)
    mod = mod[:, :batch + 1].reshape(depth, batch + 1, N_MOD, 1, d)

    def mvec(layer, k):
        return mod[layer, :, k]

    cos_t, sin_t = _rope_tables(seq, n_ctx)

    def ffn_w(layer, k):
        return (ffn_w1[layer, k].astype(BF16), ffn_w3[layer, k].astype(BF16), ffn_w2[layer, k].astype(BF16))

    xs = _ffn(x.reshape(n_lat, d), n_all, cidx, norm_g[0, 0], mvec(0, 0), mvec(0, 1), mvec(0, 2), *ffn_w(0, 0),
              x_ctx=ctx.reshape(n_ctx, d))
    q, kk, vv, u, gv = _ab_in(xs, cidx, tab_idx, norm_g[0, 1], mvec(0, 3), mvec(0, 4),
                              ab_w_in[0].astype(BF16), cos_t, sin_t)
    oa = _win_attention(q, kk, vv, a_sink[0], batch, seq, ctx_len)
    ob = _gmlp(u, gv, b_ws[0].astype(BF16), b_bias[0].T)
    xs = _ab_out(xs, oa, ob, ab_w_out[0].astype(BF16), cidx, mvec(0, 5))
    xs = _ffn(xs, n_all, cidx, norm_g[0, 2], mvec(0, 6), mvec(0, 7), mvec(0, 8), *ffn_w(0, 1))

    xs = _ffn(xs, n_all, cidx, norm_g[1, 0], mvec(1, 0), mvec(1, 1), mvec(1, 2), *ffn_w(1, 0))
    wi = cd_w_in[0]
    o_kr = C_Q_RANK + C_KV_RANK
    w_kr = wi[:, o_kr:o_kr + C_ROPE]
    w_cd = jnp.concatenate([wi[:, :o_kr], wi[:, o_kr + C_ROPE:], w_kr, w_kr], axis=1).astype(BF16)
    cq, ckv, kr2, db, t = _cd_in(xs, cidx, tab_idx, norm_g[1, 1], mvec(1, 3), mvec(1, 4), w_cd,
                                 c_q_norm[0], c_kv_norm[0], cos_t, sin_t)
    w_uq = c_w_uq[0].reshape(C_Q_RANK, C_HEADS, C_NOPE + C_ROPE)
    w_qn = w_uq[:, :, :C_NOPE].reshape(C_Q_RANK, C_HEADS * C_NOPE).astype(BF16)
    w_qr = w_uq[:, :, C_NOPE:].reshape(C_Q_RANK, C_HEADS * C_ROPE).astype(BF16)
    w_ukv = c_w_ukv[0].reshape(C_KV_RANK, C_HEADS, C_NOPE + C_V)
    w_kn = w_ukv[:, :, :C_NOPE].reshape(C_KV_RANK, C_HEADS * C_NOPE).astype(BF16)
    w_v = w_ukv[:, :, C_NOPE:].reshape(C_KV_RANK, C_HEADS * C_V).astype(BF16)
    qm = _mla_q(cq, n_lat, w_qn, w_qr, cos_t, sin_t, tab_idx)
    km, vm = _mla_kv(ckv, kr2, w_kn, w_v, batch, seq, ctx_len)
    oc = _mla_attention(qm, km, vm, batch, seq)
    xl = _cd_out(xs, n_lat, oc, db, t, d_conv_w[0], cd_w_out[0].astype(BF16), cidx, mvec(1, 5), tiles_per_seq)
    out = _ffn(xl, n_lat, cidx, norm_g[1, 2], mvec(1, 6), mvec(1, 7), mvec(1, 8), *ffn_w(1, 1),
               final_g=final_norm)
    return out.reshape(batch, seq, d)
```

```python
import functools

import jax
import jax.numpy as jnp
from jax import lax
from jax.experimental import pallas as pl
from jax.experimental.pallas import tpu as pltpu

F32 = jnp.float32
BF16 = jnp.bfloat16

GRID_W = 64
ROPE_BASE = 10000.0
ROPE_DIM = 64
EPS = 1e-6
LOG2E = 1.4426950408889634
N_MOD = 9
WINDOW = 128
A_HEADS = 16
A_KV_HEADS = 2
A_HEAD_DIM = 64
A_Q_W = A_HEADS * A_HEAD_DIM
A_KV_W = A_KV_HEADS * A_HEAD_DIM
B_GROUPS = 8
B_CHUNK = 128
B_W = 1024
C_HEADS = 8
C_NOPE = 128
C_ROPE = ROPE_DIM
C_V = 128
C_Q_RANK = 768
C_KV_RANK = 512
D_W = 1024

LANES = 128
SUBLANES = 8
ROW_TILE = 512
FFN_TILE = 512
ATT_BLOCK = 128
MLA_Q_TILE = 1024
MLA_K_TILE_MAX = 768
MLA_ROW_BLOCK = 256
MOD_TILE = 1024
VMEM_LIMIT = 56 * 1024 * 1024


def _cparams(sem):
    return pltpu.CompilerParams(dimension_semantics=sem, vmem_limit_bytes=VMEM_LIMIT)


def _resident(shape):
    nd = len(shape)
    return pl.BlockSpec(shape, lambda *_: (0,) * nd, pipeline_mode=pl.Buffered(1))


def _rms(x):
    return x * lax.rsqrt(jnp.mean(x * x, axis=-1, keepdims=True) + EPS)


def _modulate(x, g, shift, scale):
    return (_rms(x) * g) * (1.0 + scale) + shift


def _rope(z, cos, sin):
    lane = lax.broadcasted_iota(jnp.int32, z.shape, 1)
    first = (lane % 32) < 16
    partner = jnp.where(first, pltpu.roll(z, LANES - 16, 1), pltpu.roll(z, 16, 1))
    return z * cos + partner * sin


def _dot(a, b):
    return jnp.dot(a, b, preferred_element_type=F32)


def _dot_nt(a, b):
    return lax.dot_general(a, b, (((1,), (1,)), ((), ())), preferred_element_type=F32)


def _modvec_kernel(c_ref, w_ref, b_ref, o_ref):
    c = c_ref[...]
    s = c * jax.nn.sigmoid(c)
    o_ref[...] = _dot(s.astype(BF16), w_ref[...].astype(BF16)) + b_ref[...]


def _modvec(cond, mod_w, mod_b):
    depth, d, n = mod_w.shape
    rows = cond.shape[0]
    return pl.pallas_call(
        _modvec_kernel,
        grid=(depth, n // MOD_TILE),
        in_specs=[
            pl.BlockSpec((rows, d), lambda l, j: (0, 0)),
            pl.BlockSpec((None, d, MOD_TILE), lambda l, j: (l, 0, j)),
            pl.BlockSpec((None, 1, MOD_TILE), lambda l, j: (l, 0, j)),
        ],
        out_specs=pl.BlockSpec((None, rows, MOD_TILE), lambda l, j: (l, 0, j)),
        out_shape=jax.ShapeDtypeStruct((depth, rows, n), F32),
        compiler_params=_cparams(("arbitrary", "arbitrary")),
        name="modvec",
    )(cond, mod_w, mod_b.reshape(depth, 1, n))


def _ffn_kernel(*refs, n_ffn_tiles, n_pro, n_lat_tiles, final):
    refs = list(refs)
    x0_ref, xn_ref = refs[0:2]
    pos = 2
    c_ref = None
    if n_lat_tiles is not None:
        c_ref = refs[pos]
        pos += 1
    g_ref, sh_ref, sc_ref, shn_ref, scn_ref, gt_ref, w13_ref, w2_ref = refs[pos:pos + 8]
    pos += 8
    fn_ref = None
    if final:
        fn_ref = refs[pos]
        pos += 1
    o_ref, h_ref, res_ref, acc_ref = refs[pos:pos + 4]
    i = pl.program_id(0)
    j = pl.program_id(1)
    cur = i % 2
    g = g_ref[...]

    @pl.when((i == 0) & (j == 0))
    def _():
        x0 = x0_ref[...]
        res_ref[0] = x0
        h_ref[0] = _modulate(x0, g, sh_ref[...], sc_ref[...]).astype(BF16)

    h = h_ref[cur]
    ab = _dot(h, w13_ref[...])
    tf = ab.shape[1] // 2
    a = ab[:, :tf]
    b = ab[:, tf:]
    act = ((a * jax.nn.sigmoid(a)) * b).astype(BF16)
    acc_ref[...] = jnp.where(j > 0, acc_ref[...], 0.0) + _dot(act, w2_ref[...])

    pro_rows = xn_ref.shape[0]
    r0 = pl.multiple_of(jnp.minimum(j, n_pro - 1) * pro_rows, pro_rows)
    nxt = xn_ref[...]
    if c_ref is not None:
        nxt = jnp.where(i + 1 < n_lat_tiles, nxt, c_ref[...])
    res_ref[1 - cur, pl.ds(r0, pro_rows), :] = nxt
    h_ref[1 - cur, pl.ds(r0, pro_rows), :] = _modulate(nxt, g, shn_ref[...], scn_ref[...]).astype(BF16)

    @pl.when(j == n_ffn_tiles - 1)
    def _():
        y = res_ref[cur] + (0.5 * gt_ref[...]) * acc_ref[...]
        if final:
            y = _rms(y) * fn_ref[...]
        o_ref[...] = y


def _ffn(x, n_rows, cidx, g, shift, scale, gate, w13, w2, which, final_g=None, x_ctx=None):
    d = x.shape[1]
    nj = w13.shape[2]
    layer, k = which
    n_tiles = n_rows // ROW_TILE
    final = final_g is not None
    n_pro = 1
    while n_pro * 2 <= min(nj, ROW_TILE // 16):
        n_pro *= 2
    pro_rows = ROW_TILE // n_pro
    n_lat_tiles = None
    if x_ctx is not None:
        assert x_ctx.shape[0] == ROW_TILE and x.shape[0] + ROW_TILE == n_rows
        n_lat_tiles = n_tiles - 1
    last_src = (n_lat_tiles if x_ctx is not None else n_tiles) - 1

    def nxt(i):
        return jnp.minimum(i + 1, n_tiles - 1)

    def chunk(j):
        return jnp.minimum(j, n_pro - 1)

    vec = pl.BlockSpec((None, 1, d), lambda i, j: (cidx(i), 0, 0))
    vec_next = pl.BlockSpec((None, 1, d), lambda i, j: (cidx(nxt(i)), 0, 0))
    const_row = pl.BlockSpec((1, d), lambda i, j: (0, 0))
    in_specs = [
        pl.BlockSpec((ROW_TILE, d), lambda i, j: (0, 0), pipeline_mode=pl.Buffered(1)),
        pl.BlockSpec((pro_rows, d), lambda i, j: (jnp.minimum(i + 1, last_src) * n_pro + chunk(j), 0)),
    ]
    args = [x, x]
    if x_ctx is not None:
        in_specs.append(pl.BlockSpec((pro_rows, d), lambda i, j: (chunk(j), 0)))
        args.append(x_ctx)
    in_specs += [
        const_row, vec, vec, vec_next, vec_next, vec,
        pl.BlockSpec((None, None, None, d, 2 * FFN_TILE), lambda i, j: (layer, k, j, 0, 0)),
        pl.BlockSpec((None, None, FFN_TILE, d), lambda i, j: (layer, k, j, 0)),
    ]
    args += [g.reshape(1, d), shift, scale, shift, scale, gate, w13, w2]
    if final:
        in_specs.append(const_row)
        args.append(final_g.reshape(1, d))
    return pl.pallas_call(
        functools.partial(_ffn_kernel, n_ffn_tiles=nj, n_pro=n_pro, n_lat_tiles=n_lat_tiles, final=final),
        grid=(n_tiles, nj),
        in_specs=in_specs,
        out_specs=pl.BlockSpec((ROW_TILE, d), lambda i, j: (i, 0)),
        out_shape=jax.ShapeDtypeStruct((n_rows, d), F32),
        scratch_shapes=[pltpu.VMEM((2, ROW_TILE, d), BF16), pltpu.VMEM((2, ROW_TILE, d), F32),
                        pltpu.VMEM((ROW_TILE, d), F32)],
        compiler_params=_cparams(("arbitrary", "arbitrary")),
        name="ffn_final" if final else ("ffn_split" if x_ctx is not None else "ffn"),
    )(*args)


def _ab_in_kernel(x_ref, g_ref, sh_ref, sc_ref, w_ref, cos_ref, sin_ref,
                  q_ref, kk_ref, vv_ref, u_ref, gv_ref):
    h = _modulate(x_ref[...], g_ref[...], sh_ref[...], sc_ref[...]).astype(BF16)
    cos = cos_ref[...]
    sin = sin_ref[...]
    q_scale = A_HEAD_DIM ** -0.5 * LOG2E
    zq = _dot(h, w_ref[:, 0:A_Q_W])
    for p in range(A_Q_W // LANES):
        slab = _rope(zq[:, p * LANES:(p + 1) * LANES], cos, sin)
        q_ref[p] = (slab * q_scale).astype(BF16)
    zkv = _dot(h, w_ref[:, A_Q_W:A_Q_W + 2 * A_KV_W])
    k = _rope(zkv[:, 0:LANES], cos, sin)
    v = zkv[:, LANES:2 * LANES]
    kk_ref[:, 0:LANES] = k.astype(BF16)
    kk_ref[:, LANES:2 * LANES] = pltpu.roll(k, A_HEAD_DIM, 1).astype(BF16)
    vv_ref[:, 0:LANES] = v.astype(BF16)
    vv_ref[:, LANES:2 * LANES] = pltpu.roll(v, A_HEAD_DIM, 1).astype(BF16)
    off = A_Q_W + 2 * A_KV_W
    u_ref[...] = _dot(h, w_ref[:, off:off + B_W])
    gv_ref[...] = jax.nn.gelu(_dot(h, w_ref[:, off + B_W:off + 2 * B_W])).astype(BF16)


def _ab_in(x, cidx, tab_idx, g, shift, scale, w_in, cos_t, sin_t):
    n_rows, d = x.shape
    vec = pl.BlockSpec((None, 1, d), lambda i: (cidx(i), 0, 0))
    tab = pl.BlockSpec((ROW_TILE, LANES), lambda i: (tab_idx(i), 0))
    n_pairs = A_Q_W // LANES
    return pl.pallas_call(
        _ab_in_kernel,
        grid=(n_rows // ROW_TILE,),
        in_specs=[
            pl.BlockSpec((ROW_TILE, d), lambda i: (i, 0)),
            pl.BlockSpec((1, d), lambda i: (0, 0)),
            vec, vec,
            _resident(w_in.shape),
            tab, tab,
        ],
        out_specs=[
            pl.BlockSpec((n_pairs, ROW_TILE, LANES), lambda i: (0, i, 0)),
            pl.BlockSpec((ROW_TILE, 2 * LANES), lambda i: (i, 0)),
            pl.BlockSpec((ROW_TILE, 2 * LANES), lambda i: (i, 0)),
            pl.BlockSpec((ROW_TILE, B_W), lambda i: (i, 0)),
            pl.BlockSpec((ROW_TILE, B_W), lambda i: (i, 0)),
        ],
        out_shape=[
            jax.ShapeDtypeStruct((n_pairs, n_rows, LANES), BF16),
            jax.ShapeDtypeStruct((n_rows, 2 * LANES), BF16),
            jax.ShapeDtypeStruct((n_rows, 2 * LANES), BF16),
            jax.ShapeDtypeStruct((n_rows, B_W), F32),
            jax.ShapeDtypeStruct((n_rows, B_W), BF16),
        ],
        compiler_params=_cparams(("arbitrary",)),
        name="ab_in",
    )(x, g.reshape(1, d), shift, scale, w_in, cos_t, sin_t)


def _win_kernel(sink_ref, q_ref, kp_ref, kc_ref, kn_ref, kx_ref, vp_ref, vc_ref, vn_ref, vx_ref,
                o_ref, *, n_lat_blocks, seq):
    i = pl.program_id(1)
    blk = ATT_BLOCK
    n_lat_keys = 3 * blk
    pairs_per_kv = (A_HEADS // A_KV_HEADS) // 2
    kall = jnp.concatenate([kp_ref[...], kc_ref[...], kn_ref[...], kx_ref[...]], axis=0)
    vall = jnp.concatenate([vp_ref[...], vc_ref[...], vn_ref[...], vx_ref[...]], axis=0)
    n_keys = kall.shape[0]
    lo = lax.broadcasted_iota(jnp.int32, (n_keys, LANES), 1) < A_HEAD_DIM
    zero = jnp.zeros((n_keys, LANES), BF16)
    k_nat, k_swp = kall[:, 0:LANES], kall[:, LANES:2 * LANES]
    v_nat, v_swp = vall[:, 0:LANES], vall[:, LANES:2 * LANES]

    qo = lax.broadcasted_iota(jnp.int32, (blk, n_keys), 0)
    ko = lax.broadcasted_iota(jnp.int32, (blk, n_keys), 1)
    rel = ko - blk - qo
    kpos = i * blk - blk + ko
    in_window = jnp.where(jnp.abs(rel) <= WINDOW, 1, 0)
    in_range = jnp.where(kpos >= 0, 1, 0) * jnp.where(kpos < seq, 1, 0)
    latent_query = jnp.where(i < n_lat_blocks, 1, 0)
    is_ctx_key = jnp.where(ko >= n_lat_keys, 1, 0)
    valid = (in_window * in_range * latent_query + is_ctx_key) > 0
    bias = jnp.where(valid, 0.0, -1e30).astype(F32)
    bias_g = jnp.concatenate([bias] * pairs_per_kv, axis=0)

    for kv in range(A_KV_HEADS):
        qg = q_ref[kv * pairs_per_kv:(kv + 1) * pairs_per_kv].reshape(pairs_per_kv * blk, LANES)
        acc = jnp.zeros((pairs_per_kv * blk, LANES), F32)
        for par in range(2):
            src_k = k_nat if (kv + par) % 2 == 0 else k_swp
            src_v = v_nat if (kv + par) % 2 == 0 else v_swp
            ke = jnp.where(lo, src_k, zero) if par == 0 else jnp.where(lo, zero, src_k)
            ve = jnp.where(lo, src_v, zero) if par == 0 else jnp.where(lo, zero, src_v)
            s = _dot_nt(qg, ke) + bias_g
            sink_col = LOG2E * jnp.concatenate(
                [jnp.full((blk, 1), sink_ref[kv * 2 * pairs_per_kv + 2 * pp + par], F32)
                 for pp in range(pairs_per_kv)], axis=0)
            m = jnp.maximum(jnp.max(s, axis=-1, keepdims=True), sink_col)
            p = jnp.exp2(s - m)
            denom = jnp.sum(p, axis=-1, keepdims=True) + jnp.exp2(sink_col - m)
            acc = acc + _dot(p.astype(BF16), ve) / denom
        for pp in range(pairs_per_kv):
            col = (kv * pairs_per_kv + pp) * LANES
            o_ref[:, col:col + LANES] = acc[pp * blk:(pp + 1) * blk].astype(BF16)


def _win_attention(q, kk, vv, sink, batch, seq, ctx_len):
    n_pairs, n_rows, _ = q.shape
    blk = ATT_BLOCK
    nlb = seq // blk
    ncb = ctx_len // blk
    lat_blocks = batch * nlb

    def q_idx(b, i):
        return jnp.where(i < nlb, b * nlb + i, lat_blocks + b * ncb + (i - nlb))

    def k_idx(b, i, off):
        return b * nlb + jnp.clip(i + off, 0, nlb - 1)

    ctx_block0 = (batch * seq) // ctx_len
    two = 2 * LANES

    def nb(off):
        return pl.BlockSpec((blk, two), lambda b, i, s: (k_idx(b, i, off), 0))

    ctx_spec = pl.BlockSpec((ctx_len, two), lambda b, i, s: (ctx_block0 + b, 0))
    grid_spec = pltpu.PrefetchScalarGridSpec(
        num_scalar_prefetch=1,
        grid=(batch, nlb + ncb),
        in_specs=[
            pl.BlockSpec((n_pairs, blk, LANES), lambda b, i, s: (0, q_idx(b, i), 0)),
            nb(-1), nb(0), nb(1), ctx_spec,
            nb(-1), nb(0), nb(1), ctx_spec,
        ],
        out_specs=pl.BlockSpec((blk, A_Q_W), lambda b, i, s: (q_idx(b, i), 0)),
    )
    return pl.pallas_call(
        functools.partial(_win_kernel, n_lat_blocks=nlb, seq=seq),
        grid_spec=grid_spec,
        out_shape=jax.ShapeDtypeStruct((n_rows, A_Q_W), BF16),
        compiler_params=_cparams(("arbitrary", "arbitrary")),
        name="win_attn",
    )(sink, q, kk, kk, kk, kk, vv, vv, vv, vv)


def _gmlp_kernel(u_ref, gv_ref, ws_ref, bias_ref, o_ref):
    n_chunks = u_ref.shape[0] // B_CHUNK
    for g in range(B_GROUPS):
        cols = slice(g * LANES, (g + 1) * LANES)
        rhs = jnp.concatenate(
            [gv_ref[c * B_CHUNK:(c + 1) * B_CHUNK, cols] for c in range(n_chunks)], axis=1)
        mixed = _dot(ws_ref[g], rhs) + bias_ref[:, g:g + 1]
        for c in range(n_chunks):
            rows = slice(c * B_CHUNK, (c + 1) * B_CHUNK)
            gu = jax.nn.gelu(u_ref[rows, cols])
            o_ref[rows, cols] = (gu * mixed[:, c * LANES:(c + 1) * LANES]).astype(BF16)


def _gmlp(u, gv, ws, bias_t):
    n_rows = u.shape[0]
    return pl.pallas_call(
        _gmlp_kernel,
        grid=(n_rows // ROW_TILE,),
        in_specs=[
            pl.BlockSpec((ROW_TILE, B_W), lambda i: (i, 0)),
            pl.BlockSpec((ROW_TILE, B_W), lambda i: (i, 0)),
            pl.BlockSpec(ws.shape, lambda i: (0, 0, 0)),
            pl.BlockSpec(bias_t.shape, lambda i: (0, 0)),
        ],
        out_specs=pl.BlockSpec((ROW_TILE, B_W), lambda i: (i, 0)),
        out_shape=jax.ShapeDtypeStruct((n_rows, B_W), BF16),
        compiler_params=_cparams(("arbitrary",)),
        name="gmlp",
    )(u, gv, ws, bias_t)


def _ab_out_kernel(x_ref, oa_ref, ob_ref, w_ref, gt_ref, o_ref):
    half = oa_ref.shape[1]
    y = _dot(oa_ref[...], w_ref[0:half, :]) + _dot(ob_ref[...], w_ref[half:2 * half, :])
    o_ref[...] = x_ref[...] + gt_ref[...] * y


def _ab_out(x, oa, ob, w_out, cidx, gate):
    n_rows, d = x.shape
    half = oa.shape[1]
    return pl.pallas_call(
        _ab_out_kernel,
        grid=(n_rows // ROW_TILE,),
        in_specs=[
            pl.BlockSpec((ROW_TILE, d), lambda i: (i, 0)),
            pl.BlockSpec((ROW_TILE, half), lambda i: (i, 0)),
            pl.BlockSpec((ROW_TILE, half), lambda i: (i, 0)),
            _resident(w_out.shape),
            pl.BlockSpec((None, 1, d), lambda i: (cidx(i), 0, 0)),
        ],
        out_specs=pl.BlockSpec((ROW_TILE, d), lambda i: (i, 0)),
        out_shape=jax.ShapeDtypeStruct((n_rows, d), F32),
        compiler_params=_cparams(("arbitrary",)),
        name="ab_out",
    )(x, oa, ob, w_out, gate)


def _cd_in_kernel(x_ref, g_ref, sh_ref, sc_ref, w_ref, qn_ref, kvn_ref, cos_ref, sin_ref,
                  cq_ref, ckv_ref, kr_ref, db_ref, t_ref):
    h = _modulate(x_ref[...], g_ref[...], sh_ref[...], sc_ref[...]).astype(BF16)
    o0 = C_Q_RANK
    o1 = o0 + C_KV_RANK
    o2 = o1 + D_W
    o3 = o2 + D_W
    o4 = o3 + D_W
    cq_ref[...] = (_rms(_dot(h, w_ref[:, 0:o0])) * qn_ref[...]).astype(BF16)
    ckv_ref[...] = (_rms(_dot(h, w_ref[:, o0:o1])) * kvn_ref[...]).astype(BF16)
    db_ref[...] = _dot(h, w_ref[:, o1:o2])
    t_ref[...] = _dot(h, w_ref[:, o2:o3]) * _dot(h, w_ref[:, o3:o4])
    kr_ref[...] = _rope(_dot(h, w_ref[:, o4:o4 + LANES]), cos_ref[...], sin_ref[...]).astype(BF16)


def _cd_in(x, cidx, tab_idx, g, shift, scale, w_in, q_norm, kv_norm, cos_t, sin_t):
    n_rows, d = x.shape
    vec = pl.BlockSpec((None, 1, d), lambda i: (cidx(i), 0, 0))
    tab = pl.BlockSpec((ROW_TILE, LANES), lambda i: (tab_idx(i), 0))

    def rows(width):
        return pl.BlockSpec((ROW_TILE, width), lambda i: (i, 0))

    return pl.pallas_call(
        _cd_in_kernel,
        grid=(n_rows // ROW_TILE,),
        in_specs=[
            rows(d),
            pl.BlockSpec((1, d), lambda i: (0, 0)),
            vec, vec,
            _resident(w_in.shape),
            pl.BlockSpec((1, C_Q_RANK), lambda i: (0, 0)),
            pl.BlockSpec((1, C_KV_RANK), lambda i: (0, 0)),
            tab, tab,
        ],
        out_specs=[rows(C_Q_RANK), rows(C_KV_RANK), rows(LANES), rows(D_W), rows(D_W)],
        out_shape=[
            jax.ShapeDtypeStruct((n_rows, C_Q_RANK), BF16),
            jax.ShapeDtypeStruct((n_rows, C_KV_RANK), BF16),
            jax.ShapeDtypeStruct((n_rows, LANES), BF16),
            jax.ShapeDtypeStruct((n_rows, D_W), F32),
            jax.ShapeDtypeStruct((n_rows, D_W), F32),
        ],
        compiler_params=_cparams(("arbitrary",)),
        name="cd_in",
    )(x, g.reshape(1, d), shift, scale, w_in, q_norm.reshape(1, -1), kv_norm.reshape(1, -1),
      cos_t, sin_t)


def _mla_q_kernel(cq_ref, wn_ref, wr_ref, cos_ref, sin_ref, q_ref):
    cq = cq_ref[...]
    scale = (C_NOPE + C_ROPE) ** -0.5 * LOG2E
    qn = _dot(cq, wn_ref[...]) * scale
    qr = _dot(cq, wr_ref[...]) * scale
    cos = cos_ref[...]
    sin = sin_ref[...]
    lo = lax.broadcasted_iota(jnp.int32, (cq.shape[0], LANES), 1) < C_ROPE
    for pair in range(C_HEADS // 2):
        slab = _rope(qr[:, pair * LANES:(pair + 1) * LANES], cos, sin)
        for par in range(2):
            hd = 2 * pair + par
            q_ref[:, 2 * hd * LANES:(2 * hd + 1) * LANES] = (
                qn[:, hd * LANES:(hd + 1) * LANES].astype(BF16))
            rot = jnp.where(lo, slab, 0.0) if par == 0 else jnp.where(lo, 0.0, slab)
            q_ref[:, (2 * hd + 1) * LANES:(2 * hd + 2) * LANES] = rot.astype(BF16)


def _mla_q(cq, n_rows, w_qn, w_qr, cos_t, sin_t, tab_idx):
    width = 2 * LANES * C_HEADS
    tab = pl.BlockSpec((ROW_TILE, LANES), lambda i: (tab_idx(i), 0))
    return pl.pallas_call(
        _mla_q_kernel,
        grid=(n_rows // ROW_TILE,),
        in_specs=[
            pl.BlockSpec((ROW_TILE, C_Q_RANK), lambda i: (i, 0)),
            _resident(w_qn.shape), _resident(w_qr.shape),
            tab, tab,
        ],
        out_specs=pl.BlockSpec((ROW_TILE, width), lambda i: (i, 0)),
        out_shape=jax.ShapeDtypeStruct((n_rows, width), BF16),
        compiler_params=_cparams(("arbitrary",)),
        name="mla_q",
    )(cq, w_qn, w_qr, cos_t, sin_t)


def _mla_kv_kernel(ckv_ref, kr_ref, wk_ref, wv_ref, k_ref, v_ref):
    ckv = ckv_ref[...]
    kn = _dot(ckv, wk_ref[...])
    kr = kr_ref[...]
    for hd in range(C_HEADS):
        k_ref[:, 2 * hd * LANES:(2 * hd + 1) * LANES] = kn[:, hd * LANES:(hd + 1) * LANES].astype(BF16)
        k_ref[:, (2 * hd + 1) * LANES:(2 * hd + 2) * LANES] = kr
    v_ref[...] = _dot(ckv, wv_ref[...]).astype(BF16)


def _mla_kv(ckv, kr2, w_kn, w_v, batch, seq, ctx_len):
    n_rows = ckv.shape[0]
    tile = ctx_len
    tiles_per_seq = seq // tile
    n_lat_tiles = batch * tiles_per_seq
    kw = 2 * LANES * C_HEADS
    vw = C_V * C_HEADS

    def out_idx(j):
        is_lat = j < n_lat_tiles
        b = jnp.where(is_lat, j // tiles_per_seq, j - n_lat_tiles)
        r = jnp.where(is_lat, j % tiles_per_seq, tiles_per_seq)
        return b, r, 0

    return pl.pallas_call(
        _mla_kv_kernel,
        grid=(n_rows // tile,),
        in_specs=[
            pl.BlockSpec((tile, C_KV_RANK), lambda j: (j, 0)),
            pl.BlockSpec((tile, LANES), lambda j: (j, 0)),
            _resident(w_kn.shape), _resident(w_v.shape),
        ],
        out_specs=[pl.BlockSpec((None, tile, kw), out_idx),
                   pl.BlockSpec((None, tile, vw), out_idx)],
        out_shape=[jax.ShapeDtypeStruct((batch, seq + ctx_len, kw), BF16),
                   jax.ShapeDtypeStruct((batch, seq + ctx_len, vw), BF16)],
        compiler_params=_cparams(("arbitrary",)),
        name="mla_kv",
    )(ckv, kr2, w_kn, w_v)


def _mla_kernel(q_ref, k_ref, v_ref, o_ref, s_ref, m_ref, l_ref, acc_ref, *, tk, n_chunks):
    q = q_ref[...]
    reps = tk // LANES

    def rows(c):
        start = c * tk
        if not isinstance(start, int):
            start = pl.multiple_of(start, tk)
        return pl.ds(start, tk)

    def scores(c, slot):
        s_ref[slot] = _dot_nt(q, k_ref[rows(c), :])

    def absorb(c, slot):
        v = v_ref[rows(c), :]
        for rb in range(q.shape[0] // MLA_ROW_BLOCK):
            blk = slice(rb * MLA_ROW_BLOCK, (rb + 1) * MLA_ROW_BLOCK)
            s = s_ref[slot, blk, :]
            m_old = m_ref[blk, :]
            m_new = jnp.maximum(m_old, jnp.broadcast_to(jnp.max(s, axis=-1, keepdims=True), m_old.shape))
            alpha = jnp.exp2(m_old - m_new)
            p = jnp.exp2(s - jnp.concatenate([m_new] * reps, axis=1))
            l_ref[blk, :] = alpha * l_ref[blk, :] + jnp.broadcast_to(
                jnp.sum(p, axis=-1, keepdims=True), m_old.shape)
            acc_ref[blk, :] = alpha * acc_ref[blk, :] + _dot(p.astype(BF16), v)
            m_ref[blk, :] = m_new

    m_ref[...] = jnp.full(m_ref.shape, -1e30, F32)
    l_ref[...] = jnp.zeros(l_ref.shape, F32)
    acc_ref[...] = jnp.zeros(acc_ref.shape, F32)
    scores(0, 0)
    n_pairs = (n_chunks - 1) // 2

    def pair(i, carry):
        scores(2 * i + 1, 1)
        absorb(2 * i, 0)
        scores(2 * i + 2, 0)
        absorb(2 * i + 1, 1)
        return carry

    for i in range(n_pairs):
        pair(i, 0)
    done = 2 * n_pairs
    if n_chunks - done == 2:
        scores(done + 1, 1)
        absorb(done, 0)
        absorb(done + 1, 1)
    else:
        absorb(done, 0)
    o_ref[...] = (acc_ref[...] / l_ref[...]).astype(BF16)


def _mla_key_tile(n_keys):
    best = LANES
    for t in range(LANES, MLA_K_TILE_MAX + 1, LANES):
        if n_keys % t == 0:
            best = t
    return best


def _mla_attention(q, k, v, batch, seq):
    n_lat = batch * seq
    n_keys = k.shape[1]
    tq = MLA_Q_TILE
    nq = seq // tq
    tk = _mla_key_tile(n_keys)
    kw = 2 * LANES
    return pl.pallas_call(
        functools.partial(_mla_kernel, tk=tk, n_chunks=n_keys // tk),
        grid=(batch, C_HEADS, nq),
        in_specs=[
            pl.BlockSpec((tq, kw), lambda b, h, i: (b * nq + i, h)),
            pl.BlockSpec((None, n_keys, kw), lambda b, h, i: (b, 0, h)),
            pl.BlockSpec((None, n_keys, C_V), lambda b, h, i: (b, 0, h)),
        ],
        out_specs=pl.BlockSpec((tq, C_V), lambda b, h, i: (b * nq + i, h)),
        out_shape=jax.ShapeDtypeStruct((n_lat, C_V * C_HEADS), BF16),
        scratch_shapes=[pltpu.VMEM((2, tq, tk), F32),
                        pltpu.VMEM((tq, LANES), F32),
                        pltpu.VMEM((tq, LANES), F32),
                        pltpu.VMEM((tq, C_V), F32)],
        compiler_params=_cparams(("arbitrary", "arbitrary", "arbitrary")),
        name="mla_attn",
    )(q, k, v)


def _cd_out_kernel(x_ref, oc_ref, db_ref, t_ref, tp_ref, tn_ref, cw_ref, w_ref, gt_ref, o_ref,
                   *, tiles_per_seq):
    i = pl.program_id(0)
    t = t_ref[...]
    rows = t.shape[0]
    row = lax.broadcasted_iota(jnp.int32, t.shape, 0)
    pos = i % tiles_per_seq
    above = jnp.where(pos > 0, tp_ref[SUBLANES - 1:SUBLANES, :], 0.0)
    below = jnp.where(pos < tiles_per_seq - 1, tn_ref[0:1, :], 0.0)
    t_up = jnp.where(row == 0, above, pltpu.roll(t, 1, 0))
    t_dn = jnp.where(row == rows - 1, below, pltpu.roll(t, rows - 1, 0))
    conv = cw_ref[0:1, :] * t_up + cw_ref[1:2, :] * t + cw_ref[2:3, :] * t_dn
    od = (db_ref[...] * conv).astype(BF16)
    half = oc_ref.shape[1]
    y = _dot(oc_ref[...], w_ref[0:half, :]) + _dot(od, w_ref[half:2 * half, :])
    o_ref[...] = x_ref[...] + gt_ref[...] * y


def _cd_out(x, n_rows, oc, db, t, conv_w, w_out, cidx, gate, tiles_per_seq):
    d = x.shape[1]
    halo_per_tile = ROW_TILE // SUBLANES
    n_halo = t.shape[0] // SUBLANES
    return pl.pallas_call(
        functools.partial(_cd_out_kernel, tiles_per_seq=tiles_per_seq),
        grid=(n_rows // ROW_TILE,),
        in_specs=[
            pl.BlockSpec((ROW_TILE, d), lambda i: (i, 0)),
            pl.BlockSpec((ROW_TILE, D_W), lambda i: (i, 0)),
            pl.BlockSpec((ROW_TILE, D_W), lambda i: (i, 0)),
            pl.BlockSpec((ROW_TILE, D_W), lambda i: (i, 0)),
            pl.BlockSpec((SUBLANES, D_W), lambda i: (jnp.maximum(i * halo_per_tile - 1, 0), 0)),
            pl.BlockSpec((SUBLANES, D_W),
                         lambda i: (jnp.minimum((i + 1) * halo_per_tile, n_halo - 1), 0)),
            pl.BlockSpec(conv_w.shape, lambda i: (0, 0)),
            _resident(w_out.shape),
            pl.BlockSpec((None, 1, d), lambda i: (cidx(i), 0, 0)),
        ],
        out_specs=pl.BlockSpec((ROW_TILE, d), lambda i: (i, 0)),
        out_shape=jax.ShapeDtypeStruct((n_rows, d), F32),
        compiler_params=_cparams(("arbitrary",)),
        name="cd_out",
    )(x, oc, db, t, t, t, conv_w, w_out, gate)


def _rope_tables(seq, n_ctx_rows):
    t = jnp.arange(seq)
    row = (t // GRID_W).astype(F32)
    col = (t % GRID_W).astype(F32)
    axis_dim = ROPE_DIM // 2
    inv_freq = ROPE_BASE ** (-jnp.arange(0, axis_dim, 2, dtype=F32) / axis_dim)
    ang_r = row[:, None] * inv_freq
    ang_c = col[:, None] * inv_freq
    cos_h = jnp.concatenate([jnp.cos(ang_r)] * 2 + [jnp.cos(ang_c)] * 2, axis=-1)
    sin_h = jnp.concatenate([-jnp.sin(ang_r), jnp.sin(ang_r), -jnp.sin(ang_c), jnp.sin(ang_c)], axis=-1)
    reps = LANES // ROPE_DIM
    cos_t = jnp.concatenate([jnp.tile(cos_h, (1, reps)), jnp.ones((n_ctx_rows, LANES), F32)], axis=0)
    sin_t = jnp.concatenate([jnp.tile(sin_h, (1, reps)), jnp.zeros((n_ctx_rows, LANES), F32)], axis=0)
    return cos_t, sin_t


def kernel(x, c, ctx, c_ctx, mod_w, mod_b, norm_g, ffn_w1, ffn_w3, ffn_w2, ab_w_in, ab_w_out, a_sink,
           b_ws, b_bias, cd_w_in, cd_w_out, c_q_norm, c_kv_norm, c_w_uq, c_w_ukv, d_conv_w, final_norm):
    batch, seq, d = x.shape
    ctx_len = ctx.shape[1]
    depth = mod_w.shape[0]
    n_lat = batch * seq
    n_ctx = batch * ctx_len
    n_all = n_lat + n_ctx
    assert depth == 2, "layer 0 mixes with A||B, layer 1 (last) with C||D"
    assert seq % ROW_TILE == 0 and n_ctx % ROW_TILE == 0 and ROW_TILE % ctx_len == 0
    assert seq % GRID_W == 0 and seq % MLA_Q_TILE == 0 and seq % ctx_len == 0
    assert ctx_len % ATT_BLOCK == 0 and n_lat % ctx_len == 0
    assert (N_MOD * d) % MOD_TILE == 0 and ffn_w1.shape[-1] % FFN_TILE == 0 and d % LANES == 0

    tiles_per_seq = seq // ROW_TILE
    n_lat_tiles = n_lat // ROW_TILE

    def cidx(i):
        return jnp.minimum(i // tiles_per_seq, batch)

    def tab_idx(i):
        return jnp.where(i < n_lat_tiles, i % tiles_per_seq, tiles_per_seq + i - n_lat_tiles)

    cond = jnp.zeros((SUBLANES, d), F32).at[:batch].set(c).at[batch].set(c_ctx)
    mod = _modvec(cond, mod_w, mod_b)
    mod = mod[:, :batch + 1].reshape(depth, batch + 1, N_MOD, 1, d)

    def mvec(layer, k):
        return mod[layer, :, k]

    cos_t, sin_t = _rope_tables(seq, n_ctx)

    f = ffn_w1.shape[-1]
    nj = f // FFN_TILE
    w1_t = ffn_w1.reshape(depth, 2, d, nj, FFN_TILE)
    w3_t = ffn_w3.reshape(depth, 2, d, nj, FFN_TILE)
    w13 = jnp.concatenate([w1_t, w3_t], axis=-1).transpose(0, 1, 3, 2, 4).astype(BF16)
    w2b = ffn_w2.astype(BF16)

    def ffn_w(layer, k):
        return w13, w2b, (layer, k)

    xs = _ffn(x.reshape(n_lat, d), n_all, cidx, norm_g[0, 0], mvec(0, 0), mvec(0, 1), mvec(0, 2), *ffn_w(0, 0),
              x_ctx=ctx.reshape(n_ctx, d))
    q, kk, vv, u, gv = _ab_in(xs, cidx, tab_idx, norm_g[0, 1], mvec(0, 3), mvec(0, 4),
                              ab_w_in[0].astype(BF16), cos_t, sin_t)
    oa = _win_attention(q, kk, vv, a_sink[0], batch, seq, ctx_len)
    ob = _gmlp(u, gv, b_ws[0].astype(BF16), b_bias[0].T)
    xs = _ab_out(xs, oa, ob, ab_w_out[0].astype(BF16), cidx, mvec(0, 5))
    xs = _ffn(xs, n_all, cidx, norm_g[0, 2], mvec(0, 6), mvec(0, 7), mvec(0, 8), *ffn_w(0, 1))

    xs = _ffn(xs, n_all, cidx, norm_g[1, 0], mvec(1, 0), mvec(1, 1), mvec(1, 2), *ffn_w(1, 0))
    wi = cd_w_in[0]
    o_kr = C_Q_RANK + C_KV_RANK
    w_kr = wi[:, o_kr:o_kr + C_ROPE]
    w_cd = jnp.concatenate([wi[:, :o_kr], wi[:, o_kr + C_ROPE:], w_kr, w_kr], axis=1).astype(BF16)
    cq, ckv, kr2, db, t = _cd_in(xs, cidx, tab_idx, norm_g[1, 1], mvec(1, 3), mvec(1, 4), w_cd,
                                 c_q_norm[0], c_kv_norm[0], cos_t, sin_t)
    w_uq = c_w_uq[0].reshape(C_Q_RANK, C_HEADS, C_NOPE + C_ROPE)
    w_qn = w_uq[:, :, :C_NOPE].reshape(C_Q_RANK, C_HEADS * C_NOPE).astype(BF16)
    w_qr = w_uq[:, :, C_NOPE:].reshape(C_Q_RANK, C_HEADS * C_ROPE).astype(BF16)
    w_ukv = c_w_ukv[0].reshape(C_KV_RANK, C_HEADS, C_NOPE + C_V)
    w_kn = w_ukv[:, :, :C_NOPE].reshape(C_KV_RANK, C_HEADS * C_NOPE).astype(BF16)
    w_v = w_ukv[:, :, C_NOPE:].reshape(C_KV_RANK, C_HEADS * C_V).astype(BF16)
    qm = _mla_q(cq, n_lat, w_qn, w_qr, cos_t, sin_t, tab_idx)
    km, vm = _mla_kv(ckv, kr2, w_kn, w_v, batch, seq, ctx_len)
    oc = _mla_attention(qm, km, vm, batch, seq)
    xl = _cd_out(xs, n_lat, oc, db, t, d_conv_w[0], cd_w_out[0].astype(BF16), cidx, mvec(1, 5), tiles_per_seq)
    out = _ffn(xl, n_lat, cidx, norm_g[1, 2], mvec(1, 6), mvec(1, 7), mvec(1, 8), *ffn_w(1, 1),
               final_g=final_norm)
    return out.reshape(batch, seq, d)
```

```python
import functools

import jax
import jax.numpy as jnp
from jax import lax
from jax.experimental import pallas as pl
from jax.experimental.pallas import tpu as pltpu

F32 = jnp.float32
BF16 = jnp.bfloat16

GRID_W = 64
ROPE_BASE = 10000.0
ROPE_DIM = 64
EPS = 1e-6
LOG2E = 1.4426950408889634
N_MOD = 9
WINDOW = 128
A_HEADS = 16
A_KV_HEADS = 2
A_HEAD_DIM = 64
A_Q_W = A_HEADS * A_HEAD_DIM
A_KV_W = A_KV_HEADS * A_HEAD_DIM
B_GROUPS = 8
B_CHUNK = 128
B_W = 1024
C_HEADS = 8
C_NOPE = 128
C_ROPE = ROPE_DIM
C_V = 128
C_Q_RANK = 768
C_KV_RANK = 512
D_W = 1024

LANES = 128
SUBLANES = 8
ROW_TILE = 512
FFN_TILE = 512
ATT_BLOCK = 128
WIN_Q_BLOCKS = 2
MLA_Q_TILE = 1024
MLA_K_TILE_MAX = 768
MLA_ROW_BLOCK = 256
MOD_TILE = 1024
VMEM_LIMIT = 56 * 1024 * 1024


def _cparams(sem):
    return pltpu.CompilerParams(dimension_semantics=sem, vmem_limit_bytes=VMEM_LIMIT)


def _resident(shape):
    nd = len(shape)
    return pl.BlockSpec(shape, lambda *_: (0,) * nd, pipeline_mode=pl.Buffered(1))


def _rms(x):
    return x * lax.rsqrt(jnp.mean(x * x, axis=-1, keepdims=True) + EPS)


def _modulate(x, g, shift, scale):
    return (_rms(x) * g) * (1.0 + scale) + shift


def _rope(z, cos, sin):
    lane = lax.broadcasted_iota(jnp.int32, z.shape, 1)
    first = (lane % 32) < 16
    partner = jnp.where(first, pltpu.roll(z, LANES - 16, 1), pltpu.roll(z, 16, 1))
    return z * cos + partner * sin


def _dot(a, b):
    return jnp.dot(a, b, preferred_element_type=F32)


def _dot_nt(a, b):
    return lax.dot_general(a, b, (((1,), (1,)), ((), ())), preferred_element_type=F32)


def _modvec_kernel(c_ref, w_ref, b_ref, o_ref):
    c = c_ref[...]
    s = c * jax.nn.sigmoid(c)
    o_ref[...] = _dot(s.astype(BF16), w_ref[...].astype(BF16)) + b_ref[...]


def _modvec(cond, mod_w, mod_b):
    depth, d, n = mod_w.shape
    rows = cond.shape[0]
    return pl.pallas_call(
        _modvec_kernel,
        grid=(depth, n // MOD_TILE),
        in_specs=[
            pl.BlockSpec((rows, d), lambda l, j: (0, 0)),
            pl.BlockSpec((None, d, MOD_TILE), lambda l, j: (l, 0, j)),
            pl.BlockSpec((None, 1, MOD_TILE), lambda l, j: (l, 0, j)),
        ],
        out_specs=pl.BlockSpec((None, rows, MOD_TILE), lambda l, j: (l, 0, j)),
        out_shape=jax.ShapeDtypeStruct((depth, rows, n), F32),
        compiler_params=_cparams(("arbitrary", "arbitrary")),
        name="modvec",
    )(cond, mod_w, mod_b.reshape(depth, 1, n))


def _ffn_kernel(*refs, n_ffn_tiles, n_pro, n_lat_tiles, final):
    refs = list(refs)
    x0_ref, xn_ref = refs[0:2]
    pos = 2
    c_ref = None
    if n_lat_tiles is not None:
        c_ref = refs[pos]
        pos += 1
    g_ref, sh_ref, sc_ref, shn_ref, scn_ref, gt_ref, w1_ref, w3_ref, w2_ref = refs[pos:pos + 9]
    pos += 9
    fn_ref = None
    if final:
        fn_ref = refs[pos]
        pos += 1
    o_ref, h_ref, res_ref, acc_ref = refs[pos:pos + 4]
    i = pl.program_id(0)
    j = pl.program_id(1)
    cur = i % 2
    g = g_ref[...]

    @pl.when((i == 0) & (j == 0))
    def _():
        x0 = x0_ref[...]
        res_ref[0] = x0
        h_ref[0] = _modulate(x0, g, sh_ref[...], sc_ref[...]).astype(BF16)

    h = h_ref[cur]
    a = _dot(h, w1_ref[...])
    b = _dot(h, w3_ref[...])
    act = ((a * jax.nn.sigmoid(a)) * b).astype(BF16)
    acc_ref[...] = jnp.where(j > 0, acc_ref[...], 0.0) + _dot(act, w2_ref[...])

    pro_rows = xn_ref.shape[0]
    r0 = pl.multiple_of(jnp.minimum(j, n_pro - 1) * pro_rows, pro_rows)
    nxt = xn_ref[...]
    if c_ref is not None:
        nxt = jnp.where(i + 1 < n_lat_tiles, nxt, c_ref[...])
    res_ref[1 - cur, pl.ds(r0, pro_rows), :] = nxt
    h_ref[1 - cur, pl.ds(r0, pro_rows), :] = _modulate(nxt, g, shn_ref[...], scn_ref[...]).astype(BF16)

    @pl.when(j == n_ffn_tiles - 1)
    def _():
        y = res_ref[cur] + (0.5 * gt_ref[...]) * acc_ref[...]
        if final:
            y = _rms(y) * fn_ref[...]
        o_ref[...] = y


def _ffn(x, n_rows, cidx, g, shift, scale, gate, w1, w3, w2, which, final_g=None, x_ctx=None):
    d = x.shape[1]
    nj = w1.shape[-1] // FFN_TILE
    layer, k = which
    n_tiles = n_rows // ROW_TILE
    final = final_g is not None
    n_pro = 1
    while n_pro * 2 <= min(nj, ROW_TILE // 16):
        n_pro *= 2
    pro_rows = ROW_TILE // n_pro
    n_lat_tiles = None
    if x_ctx is not None:
        assert x_ctx.shape[0] == ROW_TILE and x.shape[0] + ROW_TILE == n_rows
        n_lat_tiles = n_tiles - 1
    last_src = (n_lat_tiles if x_ctx is not None else n_tiles) - 1

    def nxt(i):
        return jnp.minimum(i + 1, n_tiles - 1)

    def chunk(j):
        return jnp.minimum(j, n_pro - 1)

    vec = pl.BlockSpec((None, 1, d), lambda i, j: (cidx(i), 0, 0))
    vec_next = pl.BlockSpec((None, 1, d), lambda i, j: (cidx(nxt(i)), 0, 0))
    const_row = pl.BlockSpec((1, d), lambda i, j: (0, 0))
    in_specs = [
        pl.BlockSpec((ROW_TILE, d), lambda i, j: (0, 0), pipeline_mode=pl.Buffered(1)),
        pl.BlockSpec((pro_rows, d), lambda i, j: (jnp.minimum(i + 1, last_src) * n_pro + chunk(j), 0)),
    ]
    args = [x, x]
    if x_ctx is not None:
        in_specs.append(pl.BlockSpec((pro_rows, d), lambda i, j: (chunk(j), 0)))
        args.append(x_ctx)
    in_specs += [
        const_row, vec, vec, vec_next, vec_next, vec,
        pl.BlockSpec((None, None, d, FFN_TILE), lambda i, j: (layer, k, 0, j)),
        pl.BlockSpec((None, None, d, FFN_TILE), lambda i, j: (layer, k, 0, j)),
        pl.BlockSpec((None, None, FFN_TILE, d), lambda i, j: (layer, k, j, 0)),
    ]
    args += [g.reshape(1, d), shift, scale, shift, scale, gate, w1, w3, w2]
    if final:
        in_specs.append(const_row)
        args.append(final_g.reshape(1, d))
    return pl.pallas_call(
        functools.partial(_ffn_kernel, n_ffn_tiles=nj, n_pro=n_pro, n_lat_tiles=n_lat_tiles, final=final),
        grid=(n_tiles, nj),
        in_specs=in_specs,
        out_specs=pl.BlockSpec((ROW_TILE, d), lambda i, j: (i, 0)),
        out_shape=jax.ShapeDtypeStruct((n_rows, d), F32),
        scratch_shapes=[pltpu.VMEM((2, ROW_TILE, d), BF16), pltpu.VMEM((2, ROW_TILE, d), F32),
                        pltpu.VMEM((ROW_TILE, d), F32)],
        compiler_params=_cparams(("arbitrary", "arbitrary")),
        name="ffn_final" if final else ("ffn_split" if x_ctx is not None else "ffn"),
    )(*args)


def _ab_in_kernel(x_ref, g_ref, sh_ref, sc_ref, w_ref, cos_ref, sin_ref,
                  q_ref, kk_ref, vv_ref, u_ref, gv_ref):
    h = _modulate(x_ref[...], g_ref[...], sh_ref[...], sc_ref[...]).astype(BF16)
    cos = cos_ref[...]
    sin = sin_ref[...]
    q_scale = A_HEAD_DIM ** -0.5 * LOG2E
    zq = _dot(h, w_ref[:, 0:A_Q_W])
    for p in range(A_Q_W // LANES):
        slab = _rope(zq[:, p * LANES:(p + 1) * LANES], cos, sin)
        q_ref[p] = (slab * q_scale).astype(BF16)
    zkv = _dot(h, w_ref[:, A_Q_W:A_Q_W + 2 * A_KV_W])
    k = _rope(zkv[:, 0:LANES], cos, sin)
    v = zkv[:, LANES:2 * LANES]
    kk_ref[:, 0:LANES] = k.astype(BF16)
    kk_ref[:, LANES:2 * LANES] = pltpu.roll(k, A_HEAD_DIM, 1).astype(BF16)
    vv_ref[:, 0:LANES] = v.astype(BF16)
    vv_ref[:, LANES:2 * LANES] = pltpu.roll(v, A_HEAD_DIM, 1).astype(BF16)
    off = A_Q_W + 2 * A_KV_W
    u_ref[...] = _dot(h, w_ref[:, off:off + B_W])
    gv_ref[...] = jax.nn.gelu(_dot(h, w_ref[:, off + B_W:off + 2 * B_W])).astype(BF16)


def _ab_in(x, cidx, tab_idx, g, shift, scale, w_in, cos_t, sin_t):
    n_rows, d = x.shape
    vec = pl.BlockSpec((None, 1, d), lambda i: (cidx(i), 0, 0))
    tab = pl.BlockSpec((ROW_TILE, LANES), lambda i: (tab_idx(i), 0))
    n_pairs = A_Q_W // LANES
    return pl.pallas_call(
        _ab_in_kernel,
        grid=(n_rows // ROW_TILE,),
        in_specs=[
            pl.BlockSpec((ROW_TILE, d), lambda i: (i, 0)),
            pl.BlockSpec((1, d), lambda i: (0, 0)),
            vec, vec,
            _resident(w_in.shape),
            tab, tab,
        ],
        out_specs=[
            pl.BlockSpec((n_pairs, ROW_TILE, LANES), lambda i: (0, i, 0)),
            pl.BlockSpec((ROW_TILE, 2 * LANES), lambda i: (i, 0)),
            pl.BlockSpec((ROW_TILE, 2 * LANES), lambda i: (i, 0)),
            pl.BlockSpec((ROW_TILE, B_W), lambda i: (i, 0)),
            pl.BlockSpec((ROW_TILE, B_W), lambda i: (i, 0)),
        ],
        out_shape=[
            jax.ShapeDtypeStruct((n_pairs, n_rows, LANES), BF16),
            jax.ShapeDtypeStruct((n_rows, 2 * LANES), BF16),
            jax.ShapeDtypeStruct((n_rows, 2 * LANES), BF16),
            jax.ShapeDtypeStruct((n_rows, B_W), F32),
            jax.ShapeDtypeStruct((n_rows, B_W), BF16),
        ],
        compiler_params=_cparams(("arbitrary",)),
        name="ab_in",
    )(x, g.reshape(1, d), shift, scale, w_in, cos_t, sin_t)


def _win_kernel(sink_ref, q_ref, k0_ref, k1_ref, k2_ref, k3_ref, kx_ref, v0_ref, v1_ref, v2_ref, v3_ref, vx_ref,
                o_ref, s_ref, *, n_lat_blocks, seq):
    step = pl.program_id(1)
    blk = ATT_BLOCK
    n_lat_keys = 3 * blk
    pairs_per_kv = (A_HEADS // A_KV_HEADS) // 2
    k_blocks = [k0_ref[...], k1_ref[...], k2_ref[...], k3_ref[...]]
    v_blocks = [v0_ref[...], v1_ref[...], v2_ref[...], v3_ref[...]]
    n_keys = n_lat_keys + kx_ref.shape[0]
    lo = lax.broadcasted_iota(jnp.int32, (n_keys, LANES), 1) < A_HEAD_DIM
    zero = jnp.zeros((n_keys, LANES), BF16)
    qo = lax.broadcasted_iota(jnp.int32, (blk, n_keys), 0)
    ko = lax.broadcasted_iota(jnp.int32, (blk, n_keys), 1)
    rel = ko - blk - qo
    in_window = jnp.where(jnp.abs(rel) <= WINDOW, 1, 0)
    is_ctx_key = jnp.where(ko >= n_lat_keys, 1, 0)

    kalls, valls, biases = [], [], []
    for qb in range(WIN_Q_BLOCKS):
        kalls.append(jnp.concatenate(k_blocks[qb:qb + 3] + [kx_ref[...]], axis=0))
        valls.append(jnp.concatenate(v_blocks[qb:qb + 3] + [vx_ref[...]], axis=0))
        i = step * WIN_Q_BLOCKS + qb
        kpos = i * blk - blk + ko
        in_range = jnp.where(kpos >= 0, 1, 0) * jnp.where(kpos < seq, 1, 0)
        latent_query = jnp.where(i < n_lat_blocks, 1, 0)
        valid = (in_window * in_range * latent_query + is_ctx_key) > 0
        bias = jnp.where(valid, 0.0, -1e30).astype(F32)
        biases.append(jnp.concatenate([bias] * pairs_per_kv, axis=0))

    units = [(qb, kv, par) for qb in range(WIN_Q_BLOCKS) for kv in range(A_KV_HEADS) for par in range(2)]

    def lane_half(x, kv, par):
        src = x[:, 0:LANES] if (kv + par) % 2 == 0 else x[:, LANES:2 * LANES]
        return jnp.where(lo, src, zero) if par == 0 else jnp.where(lo, zero, src)

    def scores(u):
        qb, kv, par = units[u]
        qg = q_ref[kv * pairs_per_kv:(kv + 1) * pairs_per_kv, qb * blk:(qb + 1) * blk, :]
        qg = qg.reshape(pairs_per_kv * blk, LANES)
        s_ref[u % 2] = _dot_nt(qg, lane_half(kalls[qb], kv, par))

    def finish(u):
        qb, kv, par = units[u]
        s = s_ref[u % 2] + biases[qb]
        sink_col = LOG2E * jnp.concatenate(
            [jnp.full((blk, 1), sink_ref[kv * 2 * pairs_per_kv + 2 * pp + par], F32)
             for pp in range(pairs_per_kv)], axis=0)
        m = jnp.maximum(jnp.max(s, axis=-1, keepdims=True), sink_col)
        p = jnp.exp2(s - m)
        denom = jnp.sum(p, axis=-1, keepdims=True) + jnp.exp2(sink_col - m)
        return _dot(p.astype(BF16), lane_half(valls[qb], kv, par)) / denom

    scores(0)
    acc = None
    for u, (qb, kv, par) in enumerate(units):
        if u + 1 < len(units):
            scores(u + 1)
        out = finish(u)
        acc = out if par == 0 else acc + out
        if par == 1:
            for pp in range(pairs_per_kv):
                col = (kv * pairs_per_kv + pp) * LANES
                o_ref[qb * blk:(qb + 1) * blk, col:col + LANES] = acc[pp * blk:(pp + 1) * blk].astype(BF16)


def _win_attention(q, kk, vv, sink, batch, seq, ctx_len):
    n_pairs, n_rows, _ = q.shape
    blk = ATT_BLOCK
    nq = WIN_Q_BLOCKS
    nlb = seq // blk
    ncb = ctx_len // blk
    assert nq == 2 and nlb % nq == 0 and ncb % nq == 0
    lat_steps = nlb // nq
    ctx_steps = ncb // nq

    def q_idx(b, i):
        return jnp.where(i < lat_steps, b * lat_steps + i, batch * lat_steps + b * ctx_steps + (i - lat_steps))

    def k_idx(b, i, j):
        return b * nlb + jnp.clip(i * nq - 1 + j, 0, nlb - 1)

    ctx_block0 = (batch * seq) // ctx_len
    two = 2 * LANES

    def nb(j):
        return pl.BlockSpec((blk, two), lambda b, i, s: (k_idx(b, i, j), 0))

    ctx_spec = pl.BlockSpec((ctx_len, two), lambda b, i, s: (ctx_block0 + b, 0))
    neighbours = [nb(j) for j in range(nq + 2)]
    grid_spec = pltpu.PrefetchScalarGridSpec(
        num_scalar_prefetch=1,
        grid=(batch, lat_steps + ctx_steps),
        in_specs=[pl.BlockSpec((n_pairs, nq * blk, LANES), lambda b, i, s: (0, q_idx(b, i), 0))]
        + neighbours + [ctx_spec] + neighbours + [ctx_spec],
        out_specs=pl.BlockSpec((nq * blk, A_Q_W), lambda b, i, s: (q_idx(b, i), 0)),
        scratch_shapes=[pltpu.VMEM((2, (n_pairs // A_KV_HEADS) * blk, 3 * blk + ctx_len), F32)],
    )
    return pl.pallas_call(
        functools.partial(_win_kernel, n_lat_blocks=nlb, seq=seq),
        grid_spec=grid_spec,
        out_shape=jax.ShapeDtypeStruct((n_rows, A_Q_W), BF16),
        compiler_params=_cparams(("arbitrary", "arbitrary")),
        name="win_attn",
    )(sink, q, *([kk] * (nq + 3)), *([vv] * (nq + 3)))


def _gmlp_kernel(u_ref, gv_ref, ws_ref, bias_ref, o_ref):
    n_chunks = u_ref.shape[0] // B_CHUNK
    for g in range(B_GROUPS):
        cols = slice(g * LANES, (g + 1) * LANES)
        rhs = jnp.concatenate(
            [gv_ref[c * B_CHUNK:(c + 1) * B_CHUNK, cols] for c in range(n_chunks)], axis=1)
        mixed = _dot(ws_ref[g], rhs) + bias_ref[:, g:g + 1]
        for c in range(n_chunks):
            rows = slice(c * B_CHUNK, (c + 1) * B_CHUNK)
            gu = jax.nn.gelu(u_ref[rows, cols])
            o_ref[rows, cols] = (gu * mixed[:, c * LANES:(c + 1) * LANES]).astype(BF16)


def _gmlp(u, gv, ws, bias_t):
    n_rows = u.shape[0]
    return pl.pallas_call(
        _gmlp_kernel,
        grid=(n_rows // ROW_TILE,),
        in_specs=[
            pl.BlockSpec((ROW_TILE, B_W), lambda i: (i, 0)),
            pl.BlockSpec((ROW_TILE, B_W), lambda i: (i, 0)),
            pl.BlockSpec(ws.shape, lambda i: (0, 0, 0)),
            pl.BlockSpec(bias_t.shape, lambda i: (0, 0)),
        ],
        out_specs=pl.BlockSpec((ROW_TILE, B_W), lambda i: (i, 0)),
        out_shape=jax.ShapeDtypeStruct((n_rows, B_W), BF16),
        compiler_params=_cparams(("arbitrary",)),
        name="gmlp",
    )(u, gv, ws, bias_t)


def _ab_out_kernel(x_ref, oa_ref, ob_ref, w_ref, gt_ref, o_ref):
    half = oa_ref.shape[1]
    y = _dot(oa_ref[...], w_ref[0:half, :]) + _dot(ob_ref[...], w_ref[half:2 * half, :])
    o_ref[...] = x_ref[...] + gt_ref[...] * y


def _ab_out(x, oa, ob, w_out, cidx, gate):
    n_rows, d = x.shape
    half = oa.shape[1]
    return pl.pallas_call(
        _ab_out_kernel,
        grid=(n_rows // ROW_TILE,),
        in_specs=[
            pl.BlockSpec((ROW_TILE, d), lambda i: (i, 0)),
            pl.BlockSpec((ROW_TILE, half), lambda i: (i, 0)),
            pl.BlockSpec((ROW_TILE, half), lambda i: (i, 0)),
            _resident(w_out.shape),
            pl.BlockSpec((None, 1, d), lambda i: (cidx(i), 0, 0)),
        ],
        out_specs=pl.BlockSpec((ROW_TILE, d), lambda i: (i, 0)),
        out_shape=jax.ShapeDtypeStruct((n_rows, d), F32),
        compiler_params=_cparams(("arbitrary",)),
        name="ab_out",
    )(x, oa, ob, w_out, gate)


def _cd_in_kernel(x_ref, g_ref, sh_ref, sc_ref, w_ref, qn_ref, kvn_ref, cos_ref, sin_ref,
                  cq_ref, ckv_ref, kr_ref, db_ref, t_ref):
    h = _modulate(x_ref[...], g_ref[...], sh_ref[...], sc_ref[...]).astype(BF16)
    o0 = C_Q_RANK
    o1 = o0 + C_KV_RANK
    o2 = o1 + D_W
    o3 = o2 + D_W
    o4 = o3 + D_W
    cq_ref[...] = (_rms(_dot(h, w_ref[:, 0:o0])) * qn_ref[...]).astype(BF16)
    ckv_ref[...] = (_rms(_dot(h, w_ref[:, o0:o1])) * kvn_ref[...]).astype(BF16)
    db_ref[...] = _dot(h, w_ref[:, o1:o2])
    t_ref[...] = _dot(h, w_ref[:, o2:o3]) * _dot(h, w_ref[:, o3:o4])
    kr_ref[...] = _rope(_dot(h, w_ref[:, o4:o4 + LANES]), cos_ref[...], sin_ref[...]).astype(BF16)


def _cd_in(x, cidx, tab_idx, g, shift, scale, w_in, q_norm, kv_norm, cos_t, sin_t):
    n_rows, d = x.shape
    vec = pl.BlockSpec((None, 1, d), lambda i: (cidx(i), 0, 0))
    tab = pl.BlockSpec((ROW_TILE, LANES), lambda i: (tab_idx(i), 0))

    def rows(width):
        return pl.BlockSpec((ROW_TILE, width), lambda i: (i, 0))

    return pl.pallas_call(
        _cd_in_kernel,
        grid=(n_rows // ROW_TILE,),
        in_specs=[
            rows(d),
            pl.BlockSpec((1, d), lambda i: (0, 0)),
            vec, vec,
            _resident(w_in.shape),
            pl.BlockSpec((1, C_Q_RANK), lambda i: (0, 0)),
            pl.BlockSpec((1, C_KV_RANK), lambda i: (0, 0)),
            tab, tab,
        ],
        out_specs=[rows(C_Q_RANK), rows(C_KV_RANK), rows(LANES), rows(D_W), rows(D_W)],
        out_shape=[
            jax.ShapeDtypeStruct((n_rows, C_Q_RANK), BF16),
            jax.ShapeDtypeStruct((n_rows, C_KV_RANK), BF16),
            jax.ShapeDtypeStruct((n_rows, LANES), BF16),
            jax.ShapeDtypeStruct((n_rows, D_W), F32),
            jax.ShapeDtypeStruct((n_rows, D_W), F32),
        ],
        compiler_params=_cparams(("arbitrary",)),
        name="cd_in",
    )(x, g.reshape(1, d), shift, scale, w_in, q_norm.reshape(1, -1), kv_norm.reshape(1, -1),
      cos_t, sin_t)


def _mla_q_kernel(cq_ref, wn_ref, wr_ref, cos_ref, sin_ref, q_ref):
    cq = cq_ref[...]
    scale = (C_NOPE + C_ROPE) ** -0.5 * LOG2E
    qn = _dot(cq, wn_ref[...]) * scale
    qr = _dot(cq, wr_ref[...]) * scale
    cos = cos_ref[...]
    sin = sin_ref[...]
    lo = lax.broadcasted_iota(jnp.int32, (cq.shape[0], LANES), 1) < C_ROPE
    for pair in range(C_HEADS // 2):
        slab = _rope(qr[:, pair * LANES:(pair + 1) * LANES], cos, sin)
        for par in range(2):
            hd = 2 * pair + par
            q_ref[:, 2 * hd * LANES:(2 * hd + 1) * LANES] = (
                qn[:, hd * LANES:(hd + 1) * LANES].astype(BF16))
            rot = jnp.where(lo, slab, 0.0) if par == 0 else jnp.where(lo, 0.0, slab)
            q_ref[:, (2 * hd + 1) * LANES:(2 * hd + 2) * LANES] = rot.astype(BF16)


def _mla_q(cq, n_rows, w_qn, w_qr, cos_t, sin_t, tab_idx):
    width = 2 * LANES * C_HEADS
    tab = pl.BlockSpec((ROW_TILE, LANES), lambda i: (tab_idx(i), 0))
    return pl.pallas_call(
        _mla_q_kernel,
        grid=(n_rows // ROW_TILE,),
        in_specs=[
            pl.BlockSpec((ROW_TILE, C_Q_RANK), lambda i: (i, 0)),
            _resident(w_qn.shape), _resident(w_qr.shape),
            tab, tab,
        ],
        out_specs=pl.BlockSpec((ROW_TILE, width), lambda i: (i, 0)),
        out_shape=jax.ShapeDtypeStruct((n_rows, width), BF16),
        compiler_params=_cparams(("arbitrary",)),
        name="mla_q",
    )(cq, w_qn, w_qr, cos_t, sin_t)


def _mla_kv_kernel(ckv_ref, kr_ref, wk_ref, wv_ref, k_ref, v_ref):
    ckv = ckv_ref[...]
    kn = _dot(ckv, wk_ref[...])
    kr = kr_ref[...]
    for hd in range(C_HEADS):
        k_ref[:, 2 * hd * LANES:(2 * hd + 1) * LANES] = kn[:, hd * LANES:(hd + 1) * LANES].astype(BF16)
        k_ref[:, (2 * hd + 1) * LANES:(2 * hd + 2) * LANES] = kr
    v_ref[...] = _dot(ckv, wv_ref[...]).astype(BF16)


def _mla_kv(ckv, kr2, w_kn, w_v, batch, seq, ctx_len):
    n_rows = ckv.shape[0]
    tile = ctx_len
    tiles_per_seq = seq // tile
    n_lat_tiles = batch * tiles_per_seq
    kw = 2 * LANES * C_HEADS
    vw = C_V * C_HEADS

    def out_idx(j):
        is_lat = j < n_lat_tiles
        b = jnp.where(is_lat, j // tiles_per_seq, j - n_lat_tiles)
        r = jnp.where(is_lat, j % tiles_per_seq, tiles_per_seq)
        return b, r, 0

    return pl.pallas_call(
        _mla_kv_kernel,
        grid=(n_rows // tile,),
        in_specs=[
            pl.BlockSpec((tile, C_KV_RANK), lambda j: (j, 0)),
            pl.BlockSpec((tile, LANES), lambda j: (j, 0)),
            _resident(w_kn.shape), _resident(w_v.shape),
        ],
        out_specs=[pl.BlockSpec((None, tile, kw), out_idx),
                   pl.BlockSpec((None, tile, vw), out_idx)],
        out_shape=[jax.ShapeDtypeStruct((batch, seq + ctx_len, kw), BF16),
                   jax.ShapeDtypeStruct((batch, seq + ctx_len, vw), BF16)],
        compiler_params=_cparams(("arbitrary",)),
        name="mla_kv",
    )(ckv, kr2, w_kn, w_v)


def _mla_kernel(q_ref, k_ref, v_ref, o_ref, s_ref, m_ref, l_ref, acc_ref, *, tk, n_chunks):
    q = q_ref[...]
    reps = tk // LANES

    def rows(c):
        start = c * tk
        if not isinstance(start, int):
            start = pl.multiple_of(start, tk)
        return pl.ds(start, tk)

    def scores(c, slot):
        s_ref[slot] = _dot_nt(q, k_ref[rows(c), :])

    def absorb(c, slot):
        v = v_ref[rows(c), :]
        for rb in range(q.shape[0] // MLA_ROW_BLOCK):
            blk = slice(rb * MLA_ROW_BLOCK, (rb + 1) * MLA_ROW_BLOCK)
            s = s_ref[slot, blk, :]
            m_old = m_ref[blk, :]
            m_new = jnp.maximum(m_old, jnp.broadcast_to(jnp.max(s, axis=-1, keepdims=True), m_old.shape))
            alpha = jnp.exp2(m_old - m_new)
            p = jnp.exp2(s - jnp.concatenate([m_new] * reps, axis=1))
            l_ref[blk, :] = alpha * l_ref[blk, :] + jnp.broadcast_to(
                jnp.sum(p, axis=-1, keepdims=True), m_old.shape)
            acc_ref[blk, :] = alpha * acc_ref[blk, :] + _dot(p.astype(BF16), v)
            m_ref[blk, :] = m_new

    m_ref[...] = jnp.full(m_ref.shape, -1e30, F32)
    l_ref[...] = jnp.zeros(l_ref.shape, F32)
    acc_ref[...] = jnp.zeros(acc_ref.shape, F32)
    scores(0, 0)
    n_pairs = (n_chunks - 1) // 2

    def pair(i, carry):
        scores(2 * i + 1, 1)
        absorb(2 * i, 0)
        scores(2 * i + 2, 0)
        absorb(2 * i + 1, 1)
        return carry

    for i in range(n_pairs):
        pair(i, 0)
    done = 2 * n_pairs
    if n_chunks - done == 2:
        scores(done + 1, 1)
        absorb(done, 0)
        absorb(done + 1, 1)
    else:
        absorb(done, 0)
    o_ref[...] = (acc_ref[...] / l_ref[...]).astype(BF16)


def _mla_key_tile(n_keys):
    best = LANES
    for t in range(LANES, MLA_K_TILE_MAX + 1, LANES):
        if n_keys % t == 0:
            best = t
    return best


def _mla_attention(q, k, v, batch, seq):
    n_lat = batch * seq
    n_keys = k.shape[1]
    tq = MLA_Q_TILE
    nq = seq // tq
    tk = _mla_key_tile(n_keys)
    kw = 2 * LANES
    return pl.pallas_call(
        functools.partial(_mla_kernel, tk=tk, n_chunks=n_keys // tk),
        grid=(batch, C_HEADS, nq),
        in_specs=[
            pl.BlockSpec((tq, kw), lambda b, h, i: (b * nq + i, h)),
            pl.BlockSpec((None, n_keys, kw), lambda b, h, i: (b, 0, h)),
            pl.BlockSpec((None, n_keys, C_V), lambda b, h, i: (b, 0, h)),
        ],
        out_specs=pl.BlockSpec((tq, C_V), lambda b, h, i: (b * nq + i, h)),
        out_shape=jax.ShapeDtypeStruct((n_lat, C_V * C_HEADS), BF16),
        scratch_shapes=[pltpu.VMEM((2, tq, tk), F32),
                        pltpu.VMEM((tq, LANES), F32),
                        pltpu.VMEM((tq, LANES), F32),
                        pltpu.VMEM((tq, C_V), F32)],
        compiler_params=_cparams(("arbitrary", "arbitrary", "arbitrary")),
        name="mla_attn",
    )(q, k, v)


def _cd_out_kernel(x_ref, oc_ref, db_ref, t_ref, tp_ref, tn_ref, cw_ref, w_ref, gt_ref, o_ref,
                   *, tiles_per_seq):
    i = pl.program_id(0)
    t = t_ref[...]
    rows = t.shape[0]
    row = lax.broadcasted_iota(jnp.int32, t.shape, 0)
    pos = i % tiles_per_seq
    above = jnp.where(pos > 0, tp_ref[SUBLANES - 1:SUBLANES, :], 0.0)
    below = jnp.where(pos < tiles_per_seq - 1, tn_ref[0:1, :], 0.0)
    t_up = jnp.where(row == 0, above, pltpu.roll(t, 1, 0))
    t_dn = jnp.where(row == rows - 1, below, pltpu.roll(t, rows - 1, 0))
    conv = cw_ref[0:1, :] * t_up + cw_ref[1:2, :] * t + cw_ref[2:3, :] * t_dn
    od = (db_ref[...] * conv).astype(BF16)
    half = oc_ref.shape[1]
    y = _dot(oc_ref[...], w_ref[0:half, :]) + _dot(od, w_ref[half:2 * half, :])
    o_ref[...] = x_ref[...] + gt_ref[...] * y


def _cd_out(x, n_rows, oc, db, t, conv_w, w_out, cidx, gate, tiles_per_seq):
    d = x.shape[1]
    halo_per_tile = ROW_TILE // SUBLANES
    n_halo = t.shape[0] // SUBLANES
    return pl.pallas_call(
        functools.partial(_cd_out_kernel, tiles_per_seq=tiles_per_seq),
        grid=(n_rows // ROW_TILE,),
        in_specs=[
            pl.BlockSpec((ROW_TILE, d), lambda i: (i, 0)),
            pl.BlockSpec((ROW_TILE, D_W), lambda i: (i, 0)),
            pl.BlockSpec((ROW_TILE, D_W), lambda i: (i, 0)),
            pl.BlockSpec((ROW_TILE, D_W), lambda i: (i, 0)),
            pl.BlockSpec((SUBLANES, D_W), lambda i: (jnp.maximum(i * halo_per_tile - 1, 0), 0)),
            pl.BlockSpec((SUBLANES, D_W),
                         lambda i: (jnp.minimum((i + 1) * halo_per_tile, n_halo - 1), 0)),
            pl.BlockSpec(conv_w.shape, lambda i: (0, 0)),
            _resident(w_out.shape),
            pl.BlockSpec((None, 1, d), lambda i: (cidx(i), 0, 0)),
        ],
        out_specs=pl.BlockSpec((ROW_TILE, d), lambda i: (i, 0)),
        out_shape=jax.ShapeDtypeStruct((n_rows, d), F32),
        compiler_params=_cparams(("arbitrary",)),
        name="cd_out",
    )(x, oc, db, t, t, t, conv_w, w_out, gate)


def _rope_tables(seq, n_ctx_rows):
    t = jnp.arange(seq)
    row = (t // GRID_W).astype(F32)
    col = (t % GRID_W).astype(F32)
    axis_dim = ROPE_DIM // 2
    inv_freq = ROPE_BASE ** (-jnp.arange(0, axis_dim, 2, dtype=F32) / axis_dim)
    ang_r = row[:, None] * inv_freq
    ang_c = col[:, None] * inv_freq
    cos_h = jnp.concatenate([jnp.cos(ang_r)] * 2 + [jnp.cos(ang_c)] * 2, axis=-1)
    sin_h = jnp.concatenate([-jnp.sin(ang_r), jnp.sin(ang_r), -jnp.sin(ang_c), jnp.sin(ang_c)], axis=-1)
    reps = LANES // ROPE_DIM
    cos_t = jnp.concatenate([jnp.tile(cos_h, (1, reps)), jnp.ones((n_ctx_rows, LANES), F32)], axis=0)
    sin_t = jnp.concatenate([jnp.tile(sin_h, (1, reps)), jnp.zeros((n_ctx_rows, LANES), F32)], axis=0)
    return cos_t, sin_t


def kernel(x, c, ctx, c_ctx, mod_w, mod_b, norm_g, ffn_w1, ffn_w3, ffn_w2, ab_w_in, ab_w_out, a_sink,
           b_ws, b_bias, cd_w_in, cd_w_out, c_q_norm, c_kv_norm, c_w_uq, c_w_ukv, d_conv_w, final_norm):
    batch, seq, d = x.shape
    ctx_len = ctx.shape[1]
    depth = mod_w.shape[0]
    n_lat = batch * seq
    n_ctx = batch * ctx_len
    n_all = n_lat + n_ctx
    assert depth == 2, "layer 0 mixes with A||B, layer 1 (last) with C||D"
    assert seq % ROW_TILE == 0 and n_ctx % ROW_TILE == 0 and ROW_TILE % ctx_len == 0
    assert seq % GRID_W == 0 and seq % MLA_Q_TILE == 0 and seq % ctx_len == 0
    assert ctx_len % ATT_BLOCK == 0 and n_lat % ctx_len == 0
    assert (N_MOD * d) % MOD_TILE == 0 and ffn_w1.shape[-1] % FFN_TILE == 0 and d % LANES == 0

    tiles_per_seq = seq // ROW_TILE
    n_lat_tiles = n_lat // ROW_TILE

    def cidx(i):
        return jnp.minimum(i // tiles_per_seq, batch)

    def tab_idx(i):
        return jnp.where(i < n_lat_tiles, i % tiles_per_seq, tiles_per_seq + i - n_lat_tiles)

    cond = jnp.zeros((SUBLANES, d), F32).at[:batch].set(c).at[batch].set(c_ctx)
    mod = _modvec(cond, mod_w, mod_b)
    mod = mod[:, :batch + 1].reshape(depth, batch + 1, N_MOD, 1, d)

    def mvec(layer, k):
        return mod[layer, :, k]

    cos_t, sin_t = _rope_tables(seq, n_ctx)

    w1b = ffn_w1.astype(BF16)
    w3b = ffn_w3.astype(BF16)
    w2b = ffn_w2.astype(BF16)

    def ffn_w(layer, k):
        return w1b, w3b, w2b, (layer, k)

    xs = _ffn(x.reshape(n_lat, d), n_all, cidx, norm_g[0, 0], mvec(0, 0), mvec(0, 1), mvec(0, 2), *ffn_w(0, 0),
              x_ctx=ctx.reshape(n_ctx, d))
    q, kk, vv, u, gv = _ab_in(xs, cidx, tab_idx, norm_g[0, 1], mvec(0, 3), mvec(0, 4),
                              ab_w_in[0].astype(BF16), cos_t, sin_t)
    oa = _win_attention(q, kk, vv, a_sink[0], batch, seq, ctx_len)
    ob = _gmlp(u, gv, b_ws[0].astype(BF16), b_bias[0].T)
    xs = _ab_out(xs, oa, ob, ab_w_out[0].astype(BF16), cidx, mvec(0, 5))
    xs = _ffn(xs, n_all, cidx, norm_g[0, 2], mvec(0, 6), mvec(0, 7), mvec(0, 8), *ffn_w(0, 1))

    xs = _ffn(xs, n_all, cidx, norm_g[1, 0], mvec(1, 0), mvec(1, 1), mvec(1, 2), *ffn_w(1, 0))
    wi = cd_w_in[0]
    o_kr = C_Q_RANK + C_KV_RANK
    w_kr = wi[:, o_kr:o_kr + C_ROPE]
    w_cd = jnp.concatenate([wi[:, :o_kr], wi[:, o_kr + C_ROPE:], w_kr, w_kr], axis=1).astype(BF16)
    cq, ckv, kr2, db, t = _cd_in(xs, cidx, tab_idx, norm_g[1, 1], mvec(1, 3), mvec(1, 4), w_cd,
                                 c_q_norm[0], c_kv_norm[0], cos_t, sin_t)
    w_uq = c_w_uq[0].reshape(C_Q_RANK, C_HEADS, C_NOPE + C_ROPE)
    w_qn = w_uq[:, :, :C_NOPE].reshape(C_Q_RANK, C_HEADS * C_NOPE).astype(BF16)
    w_qr = w_uq[:, :, C_NOPE:].reshape(C_Q_RANK, C_HEADS * C_ROPE).astype(BF16)
    w_ukv = c_w_ukv[0].reshape(C_KV_RANK, C_HEADS, C_NOPE + C_V)
    w_kn = w_ukv[:, :, :C_NOPE].reshape(C_KV_RANK, C_HEADS * C_NOPE).astype(BF16)
    w_v = w_ukv[:, :, C_NOPE:].reshape(C_KV_RANK, C_HEADS * C_V).astype(BF16)
    qm = _mla_q(cq, n_lat, w_qn, w_qr, cos_t, sin_t, tab_idx)
    km, vm = _mla_kv(ckv, kr2, w_kn, w_v, batch, seq, ctx_len)
    oc = _mla_attention(qm, km, vm, batch, seq)
    xl = _cd_out(xs, n_lat, oc, db, t, d_conv_w[0], cd_w_out[0].astype(BF16), cidx, mvec(1, 5), tiles_per_seq)
    out = _ffn(xl, n_lat, cidx, norm_g[1, 2], mvec(1, 6), mvec(1, 7), mvec(1, 8), *ffn_w(1, 1),
               final_g=final_norm)
    return out.reshape(batch, seq, d)
```

```python
import functools

import jax
import jax.numpy as jnp
from jax import lax
from jax.experimental import pallas as pl
from jax.experimental.pallas import tpu as pltpu

F32 = jnp.float32
BF16 = jnp.bfloat16

GRID_W = 64
ROPE_BASE = 10000.0
ROPE_DIM = 64
EPS = 1e-6
LOG2E = 1.4426950408889634
N_MOD = 9
WINDOW = 128
A_HEADS = 16
A_KV_HEADS = 2
A_HEAD_DIM = 64
A_Q_W = A_HEADS * A_HEAD_DIM
A_KV_W = A_KV_HEADS * A_HEAD_DIM
B_GROUPS = 8
B_CHUNK = 128
B_W = 1024
C_HEADS = 8
C_NOPE = 128
C_ROPE = ROPE_DIM
C_V = 128
C_Q_RANK = 768
C_KV_RANK = 512
D_W = 1024

LANES = 128
SUBLANES = 8
ROW_TILE = 512
FFN_TILE = 512
ATT_BLOCK = 128
WIN_Q_BLOCKS = 2
MLA_Q_TILE = 1024
MLA_K_TILE_MAX = 768
MLA_ROW_BLOCK = 256
MOD_TILE = 1024
VMEM_LIMIT = 56 * 1024 * 1024


def _cparams(sem):
    return pltpu.CompilerParams(dimension_semantics=sem, vmem_limit_bytes=VMEM_LIMIT)


def _resident(shape):
    nd = len(shape)
    return pl.BlockSpec(shape, lambda *_: (0,) * nd, pipeline_mode=pl.Buffered(1))


def _rms(x):
    return x * lax.rsqrt(jnp.mean(x * x, axis=-1, keepdims=True) + EPS)


def _modulate(x, g, shift, scale):
    return (_rms(x) * g) * (1.0 + scale) + shift


def _rope(z, cos, sin):
    lane = lax.broadcasted_iota(jnp.int32, z.shape, 1)
    first = (lane % 32) < 16
    partner = jnp.where(first, pltpu.roll(z, LANES - 16, 1), pltpu.roll(z, 16, 1))
    return z * cos + partner * sin


def _dot(a, b):
    return jnp.dot(a, b, preferred_element_type=F32)


def _dot_nt(a, b):
    return lax.dot_general(a, b, (((1,), (1,)), ((), ())), preferred_element_type=F32)


def _modvec_kernel(c_ref, w_ref, b_ref, o_ref):
    c = c_ref[...]
    s = c * jax.nn.sigmoid(c)
    o_ref[...] = _dot(s.astype(BF16), w_ref[...].astype(BF16)) + b_ref[...]


def _modvec(cond, mod_w, mod_b):
    depth, d, n = mod_w.shape
    rows = cond.shape[0]
    return pl.pallas_call(
        _modvec_kernel,
        grid=(depth, n // MOD_TILE),
        in_specs=[
            pl.BlockSpec((rows, d), lambda l, j: (0, 0)),
            pl.BlockSpec((None, d, MOD_TILE), lambda l, j: (l, 0, j)),
            pl.BlockSpec((None, 1, MOD_TILE), lambda l, j: (l, 0, j)),
        ],
        out_specs=pl.BlockSpec((None, rows, MOD_TILE), lambda l, j: (l, 0, j)),
        out_shape=jax.ShapeDtypeStruct((depth, rows, n), F32),
        compiler_params=_cparams(("arbitrary", "arbitrary")),
        name="modvec",
    )(cond, mod_w, mod_b.reshape(depth, 1, n))


def _ffn_kernel(*refs, n_ffn_tiles, n_pro, n_tiles, n_lat_tiles, final, which):
    refs = list(refs)
    x_ref, xn_ref = refs[0:2]
    pos = 2
    c_ref = None
    if n_lat_tiles is not None:
        c_ref = refs[pos]
        pos += 1
    g_ref, sh_ref, sc_ref, shn_ref, scn_ref, gt_ref, w1_hbm, w3_hbm, w2_hbm = refs[pos:pos + 9]
    pos += 9
    fn_ref = None
    if final:
        fn_ref = refs[pos]
        pos += 1
    o_ref, h_ref, acc_ref, w1_buf, w3_buf, w2_buf, sem = refs[pos:pos + 7]
    layer, k = which
    i = pl.program_id(0)
    cur = i % 2
    g = g_ref[...]
    tf = w1_buf.shape[2]
    total = n_tiles * n_ffn_tiles
    pro_rows = x_ref.shape[0] // n_pro

    def weight_copies(j, slot):
        col = pl.multiple_of(j * tf, tf)
        return (
            pltpu.make_async_copy(w1_hbm.at[layer, k, :, pl.ds(col, tf)], w1_buf.at[slot], sem.at[slot, 0]),
            pltpu.make_async_copy(w3_hbm.at[layer, k, :, pl.ds(col, tf)], w3_buf.at[slot], sem.at[slot, 1]),
            pltpu.make_async_copy(w2_hbm.at[layer, k, pl.ds(col, tf), :], w2_buf.at[slot], sem.at[slot, 2]),
        )

    def residual():
        res = x_ref[...]
        if c_ref is not None:
            res = jnp.where(i < n_lat_tiles, res, c_ref[...])
        return res

    @pl.when(i == 0)
    def _():
        for cp in weight_copies(0, 0):
            cp.start()
        h_ref[0] = _modulate(x_ref[...], g, sh_ref[...], sc_ref[...]).astype(BF16)

    def hidden_tile(j, carry):
        step = i * n_ffn_tiles + j
        slot = step % 2
        for cp in weight_copies(j, slot):
            cp.wait()

        @pl.when(step + 1 < total)
        def _():
            for cp in weight_copies(jnp.where(j + 1 < n_ffn_tiles, j + 1, 0), 1 - slot):
                cp.start()

        h = h_ref[cur]
        a = _dot(h, w1_buf[slot])
        b = _dot(h, w3_buf[slot])
        act = ((a * jax.nn.sigmoid(a)) * b).astype(BF16)
        acc_ref[...] = jnp.where(j > 0, acc_ref[...], 0.0) + _dot(act, w2_buf[slot])

        r0 = pl.multiple_of(jnp.minimum(j, n_pro - 1) * pro_rows, pro_rows)
        nxt = xn_ref[pl.ds(r0, pro_rows), :]
        if c_ref is not None:
            nxt = jnp.where(i + 1 < n_lat_tiles, nxt, c_ref[pl.ds(r0, pro_rows), :])
        h_ref[1 - cur, pl.ds(r0, pro_rows), :] = _modulate(nxt, g, shn_ref[...], scn_ref[...]).astype(BF16)
        return carry

    lax.fori_loop(0, n_ffn_tiles, hidden_tile, 0)

    y = residual() + (0.5 * gt_ref[...]) * acc_ref[...]
    if final:
        y = _rms(y) * fn_ref[...]
    o_ref[...] = y


def _ffn(x, n_rows, cidx, g, shift, scale, gate, w1, w3, w2, which, final_g=None, x_ctx=None):
    d = x.shape[1]
    nj = w1.shape[-1] // FFN_TILE
    n_tiles = n_rows // ROW_TILE
    final = final_g is not None
    n_pro = 1
    while n_pro * 2 <= min(nj, ROW_TILE // 16):
        n_pro *= 2
    n_lat_tiles = None
    if x_ctx is not None:
        assert x_ctx.shape[0] == ROW_TILE and x.shape[0] + ROW_TILE == n_rows
        n_lat_tiles = n_tiles - 1
    last_src = (n_lat_tiles if x_ctx is not None else n_tiles) - 1

    def nxt(i):
        return jnp.minimum(i + 1, n_tiles - 1)

    vec = pl.BlockSpec((None, 1, d), lambda i: (cidx(i), 0, 0))
    vec_next = pl.BlockSpec((None, 1, d), lambda i: (cidx(nxt(i)), 0, 0))
    const_row = pl.BlockSpec((1, d), lambda i: (0, 0))
    hbm = pl.BlockSpec(memory_space=pl.ANY)
    in_specs = [
        pl.BlockSpec((ROW_TILE, d), lambda i: (jnp.minimum(i, last_src), 0)),
        pl.BlockSpec((ROW_TILE, d), lambda i: (jnp.minimum(i + 1, last_src), 0)),
    ]
    args = [x, x]
    if x_ctx is not None:
        in_specs.append(pl.BlockSpec((ROW_TILE, d), lambda i: (0, 0), pipeline_mode=pl.Buffered(1)))
        args.append(x_ctx)
    in_specs += [const_row, vec, vec, vec_next, vec_next, vec, hbm, hbm, hbm]
    args += [g.reshape(1, d), shift, scale, shift, scale, gate, w1, w3, w2]
    if final:
        in_specs.append(const_row)
        args.append(final_g.reshape(1, d))
    return pl.pallas_call(
        functools.partial(_ffn_kernel, n_ffn_tiles=nj, n_pro=n_pro, n_tiles=n_tiles, n_lat_tiles=n_lat_tiles,
                          final=final, which=which),
        grid=(n_tiles,),
        in_specs=in_specs,
        out_specs=pl.BlockSpec((ROW_TILE, d), lambda i: (i, 0)),
        out_shape=jax.ShapeDtypeStruct((n_rows, d), F32),
        scratch_shapes=[pltpu.VMEM((2, ROW_TILE, d), BF16), pltpu.VMEM((ROW_TILE, d), F32),
                        pltpu.VMEM((2, d, FFN_TILE), BF16), pltpu.VMEM((2, d, FFN_TILE), BF16),
                        pltpu.VMEM((2, FFN_TILE, d), BF16), pltpu.SemaphoreType.DMA((2, 3))],
        compiler_params=_cparams(("arbitrary",)),
        name="ffn_final" if final else ("ffn_split" if x_ctx is not None else "ffn"),
    )(*args)


def _ab_in_kernel(x_ref, g_ref, sh_ref, sc_ref, w_ref, cos_ref, sin_ref,
                  q_ref, kk_ref, vv_ref, u_ref, gv_ref):
    h = _modulate(x_ref[...], g_ref[...], sh_ref[...], sc_ref[...]).astype(BF16)
    cos = cos_ref[...]
    sin = sin_ref[...]
    q_scale = A_HEAD_DIM ** -0.5 * LOG2E
    zq = _dot(h, w_ref[:, 0:A_Q_W])
    for p in range(A_Q_W // LANES):
        slab = _rope(zq[:, p * LANES:(p + 1) * LANES], cos, sin)
        q_ref[p] = (slab * q_scale).astype(BF16)
    zkv = _dot(h, w_ref[:, A_Q_W:A_Q_W + 2 * A_KV_W])
    k = _rope(zkv[:, 0:LANES], cos, sin)
    v = zkv[:, LANES:2 * LANES]
    kk_ref[:, 0:LANES] = k.astype(BF16)
    kk_ref[:, LANES:2 * LANES] = pltpu.roll(k, A_HEAD_DIM, 1).astype(BF16)
    vv_ref[:, 0:LANES] = v.astype(BF16)
    vv_ref[:, LANES:2 * LANES] = pltpu.roll(v, A_HEAD_DIM, 1).astype(BF16)
    off = A_Q_W + 2 * A_KV_W
    u_ref[...] = _dot(h, w_ref[:, off:off + B_W])
    gv_ref[...] = jax.nn.gelu(_dot(h, w_ref[:, off + B_W:off + 2 * B_W])).astype(BF16)


def _ab_in(x, cidx, tab_idx, g, shift, scale, w_in, cos_t, sin_t):
    n_rows, d = x.shape
    vec = pl.BlockSpec((None, 1, d), lambda i: (cidx(i), 0, 0))
    tab = pl.BlockSpec((ROW_TILE, LANES), lambda i: (tab_idx(i), 0))
    n_pairs = A_Q_W // LANES
    return pl.pallas_call(
        _ab_in_kernel,
        grid=(n_rows // ROW_TILE,),
        in_specs=[
            pl.BlockSpec((ROW_TILE, d), lambda i: (i, 0)),
            pl.BlockSpec((1, d), lambda i: (0, 0)),
            vec, vec,
            _resident(w_in.shape),
            tab, tab,
        ],
        out_specs=[
            pl.BlockSpec((n_pairs, ROW_TILE, LANES), lambda i: (0, i, 0)),
            pl.BlockSpec((ROW_TILE, 2 * LANES), lambda i: (i, 0)),
            pl.BlockSpec((ROW_TILE, 2 * LANES), lambda i: (i, 0)),
            pl.BlockSpec((ROW_TILE, B_W), lambda i: (i, 0)),
            pl.BlockSpec((ROW_TILE, B_W), lambda i: (i, 0)),
        ],
        out_shape=[
            jax.ShapeDtypeStruct((n_pairs, n_rows, LANES), BF16),
            jax.ShapeDtypeStruct((n_rows, 2 * LANES), BF16),
            jax.ShapeDtypeStruct((n_rows, 2 * LANES), BF16),
            jax.ShapeDtypeStruct((n_rows, B_W), F32),
            jax.ShapeDtypeStruct((n_rows, B_W), BF16),
        ],
        compiler_params=_cparams(("arbitrary",)),
        name="ab_in",
    )(x, g.reshape(1, d), shift, scale, w_in, cos_t, sin_t)


def _win_kernel(sink_ref, q_ref, k0_ref, k1_ref, k2_ref, k3_ref, kx_ref, v0_ref, v1_ref, v2_ref, v3_ref, vx_ref,
                o_ref, s_ref, *, n_lat_blocks, seq):
    step = pl.program_id(1)
    blk = ATT_BLOCK
    n_lat_keys = 3 * blk
    pairs_per_kv = (A_HEADS // A_KV_HEADS) // 2
    k_blocks = [k0_ref[...], k1_ref[...], k2_ref[...], k3_ref[...]]
    v_blocks = [v0_ref[...], v1_ref[...], v2_ref[...], v3_ref[...]]
    n_keys = n_lat_keys + kx_ref.shape[0]
    lo = lax.broadcasted_iota(jnp.int32, (n_keys, LANES), 1) < A_HEAD_DIM
    zero = jnp.zeros((n_keys, LANES), BF16)
    qo = lax.broadcasted_iota(jnp.int32, (blk, n_keys), 0)
    ko = lax.broadcasted_iota(jnp.int32, (blk, n_keys), 1)
    rel = ko - blk - qo
    in_window = jnp.where(jnp.abs(rel) <= WINDOW, 1, 0)
    is_ctx_key = jnp.where(ko >= n_lat_keys, 1, 0)

    kalls, valls, biases = [], [], []
    for qb in range(WIN_Q_BLOCKS):
        kalls.append(jnp.concatenate(k_blocks[qb:qb + 3] + [kx_ref[...]], axis=0))
        valls.append(jnp.concatenate(v_blocks[qb:qb + 3] + [vx_ref[...]], axis=0))
        i = step * WIN_Q_BLOCKS + qb
        kpos = i * blk - blk + ko
        in_range = jnp.where(kpos >= 0, 1, 0) * jnp.where(kpos < seq, 1, 0)
        latent_query = jnp.where(i < n_lat_blocks, 1, 0)
        valid = (in_window * in_range * latent_query + is_ctx_key) > 0
        bias = jnp.where(valid, 0.0, -1e30).astype(F32)
        biases.append(jnp.concatenate([bias] * pairs_per_kv, axis=0))

    units = [(qb, kv, par) for qb in range(WIN_Q_BLOCKS) for kv in range(A_KV_HEADS) for par in range(2)]

    def lane_half(x, kv, par):
        src = x[:, 0:LANES] if (kv + par) % 2 == 0 else x[:, LANES:2 * LANES]
        return jnp.where(lo, src, zero) if par == 0 else jnp.where(lo, zero, src)

    def scores(u):
        qb, kv, par = units[u]
        qg = q_ref[kv * pairs_per_kv:(kv + 1) * pairs_per_kv, qb * blk:(qb + 1) * blk, :]
        qg = qg.reshape(pairs_per_kv * blk, LANES)
        s_ref[u % 2] = _dot_nt(qg, lane_half(kalls[qb], kv, par))

    def finish(u):
        qb, kv, par = units[u]
        s = s_ref[u % 2] + biases[qb]
        sink_col = LOG2E * jnp.concatenate(
            [jnp.full((blk, 1), sink_ref[kv * 2 * pairs_per_kv + 2 * pp + par], F32)
             for pp in range(pairs_per_kv)], axis=0)
        m = jnp.maximum(jnp.max(s, axis=-1, keepdims=True), sink_col)
        p = jnp.exp2(s - m)
        denom = jnp.sum(p, axis=-1, keepdims=True) + jnp.exp2(sink_col - m)
        return _dot(p.astype(BF16), lane_half(valls[qb], kv, par)) / denom

    scores(0)
    acc = None
    for u, (qb, kv, par) in enumerate(units):
        if u + 1 < len(units):
            scores(u + 1)
        out = finish(u)
        acc = out if par == 0 else acc + out
        if par == 1:
            for pp in range(pairs_per_kv):
                col = (kv * pairs_per_kv + pp) * LANES
                o_ref[qb * blk:(qb + 1) * blk, col:col + LANES] = acc[pp * blk:(pp + 1) * blk].astype(BF16)


def _win_attention(q, kk, vv, sink, batch, seq, ctx_len):
    n_pairs, n_rows, _ = q.shape
    blk = ATT_BLOCK
    nq = WIN_Q_BLOCKS
    nlb = seq // blk
    ncb = ctx_len // blk
    assert nq == 2 and nlb % nq == 0 and ncb % nq == 0
    lat_steps = nlb // nq
    ctx_steps = ncb // nq

    def q_idx(b, i):
        return jnp.where(i < lat_steps, b * lat_steps + i, batch * lat_steps + b * ctx_steps + (i - lat_steps))

    def k_idx(b, i, j):
        return b * nlb + jnp.clip(i * nq - 1 + j, 0, nlb - 1)

    ctx_block0 = (batch * seq) // ctx_len
    two = 2 * LANES

    def nb(j):
        return pl.BlockSpec((blk, two), lambda b, i, s: (k_idx(b, i, j), 0))

    ctx_spec = pl.BlockSpec((ctx_len, two), lambda b, i, s: (ctx_block0 + b, 0))
    neighbours = [nb(j) for j in range(nq + 2)]
    grid_spec = pltpu.PrefetchScalarGridSpec(
        num_scalar_prefetch=1,
        grid=(batch, lat_steps + ctx_steps),
        in_specs=[pl.BlockSpec((n_pairs, nq * blk, LANES), lambda b, i, s: (0, q_idx(b, i), 0))]
        + neighbours + [ctx_spec] + neighbours + [ctx_spec],
        out_specs=pl.BlockSpec((nq * blk, A_Q_W), lambda b, i, s: (q_idx(b, i), 0)),
        scratch_shapes=[pltpu.VMEM((2, (n_pairs // A_KV_HEADS) * blk, 3 * blk + ctx_len), F32)],
    )
    return pl.pallas_call(
        functools.partial(_win_kernel, n_lat_blocks=nlb, seq=seq),
        grid_spec=grid_spec,
        out_shape=jax.ShapeDtypeStruct((n_rows, A_Q_W), BF16),
        compiler_params=_cparams(("arbitrary", "arbitrary")),
        name="win_attn",
    )(sink, q, *([kk] * (nq + 3)), *([vv] * (nq + 3)))


def _gmlp_kernel(u_ref, gv_ref, ws_ref, bias_ref, o_ref):
    n_chunks = u_ref.shape[0] // B_CHUNK
    for g in range(B_GROUPS):
        cols = slice(g * LANES, (g + 1) * LANES)
        rhs = jnp.concatenate(
            [gv_ref[c * B_CHUNK:(c + 1) * B_CHUNK, cols] for c in range(n_chunks)], axis=1)
        mixed = _dot(ws_ref[g], rhs) + bias_ref[:, g:g + 1]
        for c in range(n_chunks):
            rows = slice(c * B_CHUNK, (c + 1) * B_CHUNK)
            gu = jax.nn.gelu(u_ref[rows, cols])
            o_ref[rows, cols] = (gu * mixed[:, c * LANES:(c + 1) * LANES]).astype(BF16)


def _gmlp(u, gv, ws, bias_t):
    n_rows = u.shape[0]
    return pl.pallas_call(
        _gmlp_kernel,
        grid=(n_rows // ROW_TILE,),
        in_specs=[
            pl.BlockSpec((ROW_TILE, B_W), lambda i: (i, 0)),
            pl.BlockSpec((ROW_TILE, B_W), lambda i: (i, 0)),
            pl.BlockSpec(ws.shape, lambda i: (0, 0, 0)),
            pl.BlockSpec(bias_t.shape, lambda i: (0, 0)),
        ],
        out_specs=pl.BlockSpec((ROW_TILE, B_W), lambda i: (i, 0)),
        out_shape=jax.ShapeDtypeStruct((n_rows, B_W), BF16),
        compiler_params=_cparams(("arbitrary",)),
        name="gmlp",
    )(u, gv, ws, bias_t)


def _ab_out_kernel(x_ref, oa_ref, ob_ref, w_ref, gt_ref, o_ref):
    half = oa_ref.shape[1]
    y = _dot(oa_ref[...], w_ref[0:half, :]) + _dot(ob_ref[...], w_ref[half:2 * half, :])
    o_ref[...] = x_ref[...] + gt_ref[...] * y


def _ab_out(x, oa, ob, w_out, cidx, gate):
    n_rows, d = x.shape
    half = oa.shape[1]
    return pl.pallas_call(
        _ab_out_kernel,
        grid=(n_rows // ROW_TILE,),
        in_specs=[
            pl.BlockSpec((ROW_TILE, d), lambda i: (i, 0)),
            pl.BlockSpec((ROW_TILE, half), lambda i: (i, 0)),
            pl.BlockSpec((ROW_TILE, half), lambda i: (i, 0)),
            _resident(w_out.shape),
            pl.BlockSpec((None, 1, d), lambda i: (cidx(i), 0, 0)),
        ],
        out_specs=pl.BlockSpec((ROW_TILE, d), lambda i: (i, 0)),
        out_shape=jax.ShapeDtypeStruct((n_rows, d), F32),
        compiler_params=_cparams(("arbitrary",)),
        name="ab_out",
    )(x, oa, ob, w_out, gate)


def _cd_in_kernel(x_ref, g_ref, sh_ref, sc_ref, w_ref, qn_ref, kvn_ref, cos_ref, sin_ref,
                  cq_ref, ckv_ref, kr_ref, db_ref, t_ref):
    h = _modulate(x_ref[...], g_ref[...], sh_ref[...], sc_ref[...]).astype(BF16)
    o0 = C_Q_RANK
    o1 = o0 + C_KV_RANK
    o2 = o1 + D_W
    o3 = o2 + D_W
    o4 = o3 + D_W
    cq_ref[...] = (_rms(_dot(h, w_ref[:, 0:o0])) * qn_ref[...]).astype(BF16)
    ckv_ref[...] = (_rms(_dot(h, w_ref[:, o0:o1])) * kvn_ref[...]).astype(BF16)
    db_ref[...] = _dot(h, w_ref[:, o1:o2])
    t_ref[...] = _dot(h, w_ref[:, o2:o3]) * _dot(h, w_ref[:, o3:o4])
    kr_ref[...] = _rope(_dot(h, w_ref[:, o4:o4 + LANES]), cos_ref[...], sin_ref[...]).astype(BF16)


def _cd_in(x, cidx, tab_idx, g, shift, scale, w_in, q_norm, kv_norm, cos_t, sin_t):
    n_rows, d = x.shape
    vec = pl.BlockSpec((None, 1, d), lambda i: (cidx(i), 0, 0))
    tab = pl.BlockSpec((ROW_TILE, LANES), lambda i: (tab_idx(i), 0))

    def rows(width):
        return pl.BlockSpec((ROW_TILE, width), lambda i: (i, 0))

    return pl.pallas_call(
        _cd_in_kernel,
        grid=(n_rows // ROW_TILE,),
        in_specs=[
            rows(d),
            pl.BlockSpec((1, d), lambda i: (0, 0)),
            vec, vec,
            _resident(w_in.shape),
            pl.BlockSpec((1, C_Q_RANK), lambda i: (0, 0)),
            pl.BlockSpec((1, C_KV_RANK), lambda i: (0, 0)),
            tab, tab,
        ],
        out_specs=[rows(C_Q_RANK), rows(C_KV_RANK), rows(LANES), rows(D_W), rows(D_W)],
        out_shape=[
            jax.ShapeDtypeStruct((n_rows, C_Q_RANK), BF16),
            jax.ShapeDtypeStruct((n_rows, C_KV_RANK), BF16),
            jax.ShapeDtypeStruct((n_rows, LANES), BF16),
            jax.ShapeDtypeStruct((n_rows, D_W), F32),
            jax.ShapeDtypeStruct((n_rows, D_W), F32),
        ],
        compiler_params=_cparams(("arbitrary",)),
        name="cd_in",
    )(x, g.reshape(1, d), shift, scale, w_in, q_norm.reshape(1, -1), kv_norm.reshape(1, -1),
      cos_t, sin_t)


def _mla_q_kernel(cq_ref, wn_ref, wr_ref, cos_ref, sin_ref, q_ref):
    cq = cq_ref[...]
    scale = (C_NOPE + C_ROPE) ** -0.5 * LOG2E
    qn = _dot(cq, wn_ref[...]) * scale
    qr = _dot(cq, wr_ref[...]) * scale
    cos = cos_ref[...]
    sin = sin_ref[...]
    lo = lax.broadcasted_iota(jnp.int32, (cq.shape[0], LANES), 1) < C_ROPE
    for pair in range(C_HEADS // 2):
        slab = _rope(qr[:, pair * LANES:(pair + 1) * LANES], cos, sin)
        for par in range(2):
            hd = 2 * pair + par
            q_ref[:, 2 * hd * LANES:(2 * hd + 1) * LANES] = (
                qn[:, hd * LANES:(hd + 1) * LANES].astype(BF16))
            rot = jnp.where(lo, slab, 0.0) if par == 0 else jnp.where(lo, 0.0, slab)
            q_ref[:, (2 * hd + 1) * LANES:(2 * hd + 2) * LANES] = rot.astype(BF16)


def _mla_q(cq, n_rows, w_qn, w_qr, cos_t, sin_t, tab_idx):
    width = 2 * LANES * C_HEADS
    tab = pl.BlockSpec((ROW_TILE, LANES), lambda i: (tab_idx(i), 0))
    return pl.pallas_call(
        _mla_q_kernel,
        grid=(n_rows // ROW_TILE,),
        in_specs=[
            pl.BlockSpec((ROW_TILE, C_Q_RANK), lambda i: (i, 0)),
            _resident(w_qn.shape), _resident(w_qr.shape),
            tab, tab,
        ],
        out_specs=pl.BlockSpec((ROW_TILE, width), lambda i: (i, 0)),
        out_shape=jax.ShapeDtypeStruct((n_rows, width), BF16),
        compiler_params=_cparams(("arbitrary",)),
        name="mla_q",
    )(cq, w_qn, w_qr, cos_t, sin_t)


def _mla_kv_kernel(ckv_ref, kr_ref, wk_ref, wv_ref, k_ref, v_ref):
    ckv = ckv_ref[...]
    kn = _dot(ckv, wk_ref[...])
    kr = kr_ref[...]
    for hd in range(C_HEADS):
        k_ref[:, 2 * hd * LANES:(2 * hd + 1) * LANES] = kn[:, hd * LANES:(hd + 1) * LANES].astype(BF16)
        k_ref[:, (2 * hd + 1) * LANES:(2 * hd + 2) * LANES] = kr
    v_ref[...] = _dot(ckv, wv_ref[...]).astype(BF16)


def _mla_kv(ckv, kr2, w_kn, w_v, batch, seq, ctx_len):
    n_rows = ckv.shape[0]
    tile = ctx_len
    tiles_per_seq = seq // tile
    n_lat_tiles = batch * tiles_per_seq
    kw = 2 * LANES * C_HEADS
    vw = C_V * C_HEADS

    def out_idx(j):
        is_lat = j < n_lat_tiles
        b = jnp.where(is_lat, j // tiles_per_seq, j - n_lat_tiles)
        r = jnp.where(is_lat, j % tiles_per_seq, tiles_per_seq)
        return b, r, 0

    return pl.pallas_call(
        _mla_kv_kernel,
        grid=(n_rows // tile,),
        in_specs=[
            pl.BlockSpec((tile, C_KV_RANK), lambda j: (j, 0)),
            pl.BlockSpec((tile, LANES), lambda j: (j, 0)),
            _resident(w_kn.shape), _resident(w_v.shape),
        ],
        out_specs=[pl.BlockSpec((None, tile, kw), out_idx),
                   pl.BlockSpec((None, tile, vw), out_idx)],
        out_shape=[jax.ShapeDtypeStruct((batch, seq + ctx_len, kw), BF16),
                   jax.ShapeDtypeStruct((batch, seq + ctx_len, vw), BF16)],
        compiler_params=_cparams(("arbitrary",)),
        name="mla_kv",
    )(ckv, kr2, w_kn, w_v)


def _mla_kernel(q_ref, k_ref, v_ref, o_ref, s_ref, m_ref, l_ref, acc_ref, *, tk, n_chunks):
    q = q_ref[...]
    reps = tk // LANES

    def rows(c):
        start = c * tk
        if not isinstance(start, int):
            start = pl.multiple_of(start, tk)
        return pl.ds(start, tk)

    def scores(c, slot):
        s_ref[slot] = _dot_nt(q, k_ref[rows(c), :])

    def absorb(c, slot):
        v = v_ref[rows(c), :]
        for rb in range(q.shape[0] // MLA_ROW_BLOCK):
            blk = slice(rb * MLA_ROW_BLOCK, (rb + 1) * MLA_ROW_BLOCK)
            s = s_ref[slot, blk, :]
            m_old = m_ref[blk, :]
            m_new = jnp.maximum(m_old, jnp.broadcast_to(jnp.max(s, axis=-1, keepdims=True), m_old.shape))
            alpha = jnp.exp2(m_old - m_new)
            p = jnp.exp2(s - jnp.concatenate([m_new] * reps, axis=1))
            l_ref[blk, :] = alpha * l_ref[blk, :] + jnp.broadcast_to(
                jnp.sum(p, axis=-1, keepdims=True), m_old.shape)
            acc_ref[blk, :] = alpha * acc_ref[blk, :] + _dot(p.astype(BF16), v)
            m_ref[blk, :] = m_new

    m_ref[...] = jnp.full(m_ref.shape, -1e30, F32)
    l_ref[...] = jnp.zeros(l_ref.shape, F32)
    acc_ref[...] = jnp.zeros(acc_ref.shape, F32)
    scores(0, 0)
    n_pairs = (n_chunks - 1) // 2

    def pair(i, carry):
        scores(2 * i + 1, 1)
        absorb(2 * i, 0)
        scores(2 * i + 2, 0)
        absorb(2 * i + 1, 1)
        return carry

    for i in range(n_pairs):
        pair(i, 0)
    done = 2 * n_pairs
    if n_chunks - done == 2:
        scores(done + 1, 1)
        absorb(done, 0)
        absorb(done + 1, 1)
    else:
        absorb(done, 0)
    o_ref[...] = (acc_ref[...] / l_ref[...]).astype(BF16)


def _mla_key_tile(n_keys):
    best = LANES
    for t in range(LANES, MLA_K_TILE_MAX + 1, LANES):
        if n_keys % t == 0:
            best = t
    return best


def _mla_attention(q, k, v, batch, seq):
    n_lat = batch * seq
    n_keys = k.shape[1]
    tq = MLA_Q_TILE
    nq = seq // tq
    tk = _mla_key_tile(n_keys)
    kw = 2 * LANES
    return pl.pallas_call(
        functools.partial(_mla_kernel, tk=tk, n_chunks=n_keys // tk),
        grid=(batch, C_HEADS, nq),
        in_specs=[
            pl.BlockSpec((tq, kw), lambda b, h, i: (b * nq + i, h)),
            pl.BlockSpec((None, n_keys, kw), lambda b, h, i: (b, 0, h)),
            pl.BlockSpec((None, n_keys, C_V), lambda b, h, i: (b, 0, h)),
        ],
        out_specs=pl.BlockSpec((tq, C_V), lambda b, h, i: (b * nq + i, h)),
        out_shape=jax.ShapeDtypeStruct((n_lat, C_V * C_HEADS), BF16),
        scratch_shapes=[pltpu.VMEM((2, tq, tk), F32),
                        pltpu.VMEM((tq, LANES), F32),
                        pltpu.VMEM((tq, LANES), F32),
                        pltpu.VMEM((tq, C_V), F32)],
        compiler_params=_cparams(("arbitrary", "arbitrary", "arbitrary")),
        name="mla_attn",
    )(q, k, v)


def _cd_out_kernel(x_ref, oc_ref, db_ref, t_ref, tp_ref, tn_ref, cw_ref, w_ref, gt_ref, o_ref,
                   *, tiles_per_seq):
    i = pl.program_id(0)
    t = t_ref[...]
    rows = t.shape[0]
    row = lax.broadcasted_iota(jnp.int32, t.shape, 0)
    pos = i % tiles_per_seq
    above = jnp.where(pos > 0, tp_ref[SUBLANES - 1:SUBLANES, :], 0.0)
    below = jnp.where(pos < tiles_per_seq - 1, tn_ref[0:1, :], 0.0)
    t_up = jnp.where(row == 0, above, pltpu.roll(t, 1, 0))
    t_dn = jnp.where(row == rows - 1, below, pltpu.roll(t, rows - 1, 0))
    conv = cw_ref[0:1, :] * t_up + cw_ref[1:2, :] * t + cw_ref[2:3, :] * t_dn
    od = (db_ref[...] * conv).astype(BF16)
    half = oc_ref.shape[1]
    y = _dot(oc_ref[...], w_ref[0:half, :]) + _dot(od, w_ref[half:2 * half, :])
    o_ref[...] = x_ref[...] + gt_ref[...] * y


def _cd_out(x, n_rows, oc, db, t, conv_w, w_out, cidx, gate, tiles_per_seq):
    d = x.shape[1]
    halo_per_tile = ROW_TILE // SUBLANES
    n_halo = t.shape[0] // SUBLANES
    return pl.pallas_call(
        functools.partial(_cd_out_kernel, tiles_per_seq=tiles_per_seq),
        grid=(n_rows // ROW_TILE,),
        in_specs=[
            pl.BlockSpec((ROW_TILE, d), lambda i: (i, 0)),
            pl.BlockSpec((ROW_TILE, D_W), lambda i: (i, 0)),
            pl.BlockSpec((ROW_TILE, D_W), lambda i: (i, 0)),
            pl.BlockSpec((ROW_TILE, D_W), lambda i: (i, 0)),
            pl.BlockSpec((SUBLANES, D_W), lambda i: (jnp.maximum(i * halo_per_tile - 1, 0), 0)),
            pl.BlockSpec((SUBLANES, D_W),
                         lambda i: (jnp.minimum((i + 1) * halo_per_tile, n_halo - 1), 0)),
            pl.BlockSpec(conv_w.shape, lambda i: (0, 0)),
            _resident(w_out.shape),
            pl.BlockSpec((None, 1, d), lambda i: (cidx(i), 0, 0)),
        ],
        out_specs=pl.BlockSpec((ROW_TILE, d), lambda i: (i, 0)),
        out_shape=jax.ShapeDtypeStruct((n_rows, d), F32),
        compiler_params=_cparams(("arbitrary",)),
        name="cd_out",
    )(x, oc, db, t, t, t, conv_w, w_out, gate)


def _rope_tables(seq, n_ctx_rows):
    t = jnp.arange(seq)
    row = (t // GRID_W).astype(F32)
    col = (t % GRID_W).astype(F32)
    axis_dim = ROPE_DIM // 2
    inv_freq = ROPE_BASE ** (-jnp.arange(0, axis_dim, 2, dtype=F32) / axis_dim)
    ang_r = row[:, None] * inv_freq
    ang_c = col[:, None] * inv_freq
    cos_h = jnp.concatenate([jnp.cos(ang_r)] * 2 + [jnp.cos(ang_c)] * 2, axis=-1)
    sin_h = jnp.concatenate([-jnp.sin(ang_r), jnp.sin(ang_r), -jnp.sin(ang_c), jnp.sin(ang_c)], axis=-1)
    reps = LANES // ROPE_DIM
    cos_t = jnp.concatenate([jnp.tile(cos_h, (1, reps)), jnp.ones((n_ctx_rows, LANES), F32)], axis=0)
    sin_t = jnp.concatenate([jnp.tile(sin_h, (1, reps)), jnp.zeros((n_ctx_rows, LANES), F32)], axis=0)
    return cos_t, sin_t


def kernel(x, c, ctx, c_ctx, mod_w, mod_b, norm_g, ffn_w1, ffn_w3, ffn_w2, ab_w_in, ab_w_out, a_sink,
           b_ws, b_bias, cd_w_in, cd_w_out, c_q_norm, c_kv_norm, c_w_uq, c_w_ukv, d_conv_w, final_norm):
    batch, seq, d = x.shape
    ctx_len = ctx.shape[1]
    depth = mod_w.shape[0]
    n_lat = batch * seq
    n_ctx = batch * ctx_len
    n_all = n_lat + n_ctx
    assert depth == 2, "layer 0 mixes with A||B, layer 1 (last) with C||D"
    assert seq % ROW_TILE == 0 and n_ctx % ROW_TILE == 0 and ROW_TILE % ctx_len == 0
    assert seq % GRID_W == 0 and seq % MLA_Q_TILE == 0 and seq % ctx_len == 0
    assert ctx_len % ATT_BLOCK == 0 and n_lat % ctx_len == 0
    assert (N_MOD * d) % MOD_TILE == 0 and ffn_w1.shape[-1] % FFN_TILE == 0 and d % LANES == 0

    tiles_per_seq = seq // ROW_TILE
    n_lat_tiles = n_lat // ROW_TILE

    def cidx(i):
        return jnp.minimum(i // tiles_per_seq, batch)

    def tab_idx(i):
        return jnp.where(i < n_lat_tiles, i % tiles_per_seq, tiles_per_seq + i - n_lat_tiles)

    cond = jnp.zeros((SUBLANES, d), F32).at[:batch].set(c).at[batch].set(c_ctx)
    mod = _modvec(cond, mod_w, mod_b)
    mod = mod[:, :batch + 1].reshape(depth, batch + 1, N_MOD, 1, d)

    def mvec(layer, k):
        return mod[layer, :, k]

    cos_t, sin_t = _rope_tables(seq, n_ctx)

    w1b = ffn_w1.astype(BF16)
    w3b = ffn_w3.astype(BF16)
    w2b = ffn_w2.astype(BF16)

    def ffn_w(layer, k):
        return w1b, w3b, w2b, (layer, k)

    xs = _ffn(x.reshape(n_lat, d), n_all, cidx, norm_g[0, 0], mvec(0, 0), mvec(0, 1), mvec(0, 2), *ffn_w(0, 0),
              x_ctx=ctx.reshape(n_ctx, d))
    q, kk, vv, u, gv = _ab_in(xs, cidx, tab_idx, norm_g[0, 1], mvec(0, 3), mvec(0, 4),
                              ab_w_in[0].astype(BF16), cos_t, sin_t)
    oa = _win_attention(q, kk, vv, a_sink[0], batch, seq, ctx_len)
    ob = _gmlp(u, gv, b_ws[0].astype(BF16), b_bias[0].T)
    xs = _ab_out(xs, oa, ob, ab_w_out[0].astype(BF16), cidx, mvec(0, 5))
    xs = _ffn(xs, n_all, cidx, norm_g[0, 2], mvec(0, 6), mvec(0, 7), mvec(0, 8), *ffn_w(0, 1))

    xs = _ffn(xs, n_all, cidx, norm_g[1, 0], mvec(1, 0), mvec(1, 1), mvec(1, 2), *ffn_w(1, 0))
    wi = cd_w_in[0]
    o_kr = C_Q_RANK + C_KV_RANK
    w_kr = wi[:, o_kr:o_kr + C_ROPE]
    w_cd = jnp.concatenate([wi[:, :o_kr], wi[:, o_kr + C_ROPE:], w_kr, w_kr], axis=1).astype(BF16)
    cq, ckv, kr2, db, t = _cd_in(xs, cidx, tab_idx, norm_g[1, 1], mvec(1, 3), mvec(1, 4), w_cd,
                                 c_q_norm[0], c_kv_norm[0], cos_t, sin_t)
    w_uq = c_w_uq[0].reshape(C_Q_RANK, C_HEADS, C_NOPE + C_ROPE)
    w_qn = w_uq[:, :, :C_NOPE].reshape(C_Q_RANK, C_HEADS * C_NOPE).astype(BF16)
    w_qr = w_uq[:, :, C_NOPE:].reshape(C_Q_RANK, C_HEADS * C_ROPE).astype(BF16)
    w_ukv = c_w_ukv[0].reshape(C_KV_RANK, C_HEADS, C_NOPE + C_V)
    w_kn = w_ukv[:, :, :C_NOPE].reshape(C_KV_RANK, C_HEADS * C_NOPE).astype(BF16)
    w_v = w_ukv[:, :, C_NOPE:].reshape(C_KV_RANK, C_HEADS * C_V).astype(BF16)
    qm = _mla_q(cq, n_lat, w_qn, w_qr, cos_t, sin_t, tab_idx)
    km, vm = _mla_kv(ckv, kr2, w_kn, w_v, batch, seq, ctx_len)
    oc = _mla_attention(qm, km, vm, batch, seq)
    xl = _cd_out(xs, n_lat, oc, db, t, d_conv_w[0], cd_w_out[0].astype(BF16), cidx, mvec(1, 5), tiles_per_seq)
    out = _ffn(xl, n_lat, cidx, norm_g[1, 2], mvec(1, 6), mvec(1, 7), mvec(1, 8), *ffn_w(1, 1),
               final_g=final_norm)
    return out.reshape(batch, seq, d)
```

```python
import functools

import jax
import jax.numpy as jnp
from jax import lax
from jax.experimental import pallas as pl
from jax.experimental.pallas import tpu as pltpu

F32 = jnp.float32
BF16 = jnp.bfloat16

GRID_W = 64
ROPE_BASE = 10000.0
ROPE_DIM = 64
EPS = 1e-6
LOG2E = 1.4426950408889634
N_MOD = 9
WINDOW = 128
A_HEADS = 16
A_KV_HEADS = 2
A_HEAD_DIM = 64
A_Q_W = A_HEADS * A_HEAD_DIM
A_KV_W = A_KV_HEADS * A_HEAD_DIM
B_GROUPS = 8
B_CHUNK = 128
B_W = 1024
C_HEADS = 8
C_NOPE = 128
C_ROPE = ROPE_DIM
C_V = 128
C_Q_RANK = 768
C_KV_RANK = 512
D_W = 1024

LANES = 128
SUBLANES = 8
ROW_TILE = 512
FFN_TILE = 512
ATT_BLOCK = 128
WIN_Q_BLOCKS = 2
MLA_Q_TILE = 1024
MLA_K_TILE_MAX = 768
MLA_ROW_BLOCK = 256
MOD_TILE = 1024
VMEM_LIMIT = 56 * 1024 * 1024


def _cparams(sem):
    return pltpu.CompilerParams(dimension_semantics=sem, vmem_limit_bytes=VMEM_LIMIT)


def _resident(shape):
    nd = len(shape)
    return pl.BlockSpec(shape, lambda *_: (0,) * nd, pipeline_mode=pl.Buffered(1))


def _rms(x):
    return x * lax.rsqrt(jnp.mean(x * x, axis=-1, keepdims=True) + EPS)


def _modulate(x, g, shift, scale):
    return (_rms(x) * g) * (1.0 + scale) + shift


def _rope(z, cos, sin):
    lane = lax.broadcasted_iota(jnp.int32, z.shape, 1)
    first = (lane % 32) < 16
    partner = jnp.where(first, pltpu.roll(z, LANES - 16, 1), pltpu.roll(z, 16, 1))
    return z * cos + partner * sin


def _dot(a, b):
    return jnp.dot(a, b, preferred_element_type=F32)


def _dot_nt(a, b):
    return lax.dot_general(a, b, (((1,), (1,)), ((), ())), preferred_element_type=F32)


def _with_weight_cast(kernel_fn, n_in, n_out):
    def wrapped(*refs):
        core_in = refs[:n_in]
        cast_in = refs[n_in:n_in + 3]
        core_out = refs[n_in + 3:n_in + 3 + n_out]
        cast_out = refs[n_in + 3 + n_out:n_in + 6 + n_out]
        kernel_fn(*core_in, *core_out, *refs[n_in + 6 + n_out:])
        for src, dst in zip(cast_in, cast_out):
            dst[...] = src[...].astype(BF16)
    return wrapped


def _weight_cast_specs(stacked, which, n_tiles):
    layer, k = which
    n_slabs = 1
    while n_slabs * 2 <= n_tiles:
        n_slabs *= 2
    in_specs, out_specs, out_shapes = [], [], []
    for w in stacked:
        rows, cols = w.shape[-2:]
        slab_rows = rows // n_slabs
        assert rows % n_slabs == 0 and slab_rows % 16 == 0
        in_specs.append(pl.BlockSpec((None, None, slab_rows, cols),
                                     lambda i: (layer, k, jnp.minimum(i, n_slabs - 1), 0)))
        out_specs.append(pl.BlockSpec((slab_rows, cols), lambda i: (i, 0)))
        out_shapes.append(jax.ShapeDtypeStruct((n_tiles * slab_rows, cols), BF16))
    return in_specs, out_specs, out_shapes


def _row_tile_call(kernel_fn, n_tiles, in_specs, args, out_specs, out_shape, name, cast=None):
    if cast is None:
        outs = pl.pallas_call(kernel_fn, grid=(n_tiles,), in_specs=in_specs, out_specs=out_specs,
                              out_shape=out_shape, compiler_params=_cparams(("arbitrary",)), name=name)(*args)
        return list(outs), None
    stacked, which = cast
    c_in, c_out, c_shape = _weight_cast_specs(stacked, which, n_tiles)
    outs = pl.pallas_call(
        _with_weight_cast(kernel_fn, len(in_specs), len(out_specs)),
        grid=(n_tiles,),
        in_specs=list(in_specs) + c_in,
        out_specs=list(out_specs) + c_out,
        out_shape=list(out_shape) + c_shape,
        compiler_params=_cparams(("arbitrary",)),
        name=name,
    )(*args, *stacked)
    n = len(out_specs)
    return list(outs[:n]), tuple(outs[n:])


def _modvec_kernel(c_ref, w_ref, b_ref, o_ref):
    c = c_ref[...]
    s = c * jax.nn.sigmoid(c)
    o_ref[...] = _dot(s.astype(BF16), w_ref[...].astype(BF16)) + b_ref[...]


def _modvec(cond, mod_w, mod_b):
    depth, d, n = mod_w.shape
    rows = cond.shape[0]
    return pl.pallas_call(
        _modvec_kernel,
        grid=(depth, n // MOD_TILE),
        in_specs=[
            pl.BlockSpec((rows, d), lambda l, j: (0, 0)),
            pl.BlockSpec((None, d, MOD_TILE), lambda l, j: (l, 0, j)),
            pl.BlockSpec((None, 1, MOD_TILE), lambda l, j: (l, 0, j)),
        ],
        out_specs=pl.BlockSpec((None, rows, MOD_TILE), lambda l, j: (l, 0, j)),
        out_shape=jax.ShapeDtypeStruct((depth, rows, n), F32),
        compiler_params=_cparams(("arbitrary", "arbitrary")),
        name="modvec",
    )(cond, mod_w, mod_b.reshape(depth, 1, n))


def _ffn_kernel(*refs, n_ffn_tiles, n_pro, n_lat_tiles, final):
    refs = list(refs)
    x0_ref, xn_ref = refs[0:2]
    pos = 2
    c_ref = None
    if n_lat_tiles is not None:
        c_ref = refs[pos]
        pos += 1
    g_ref, sh_ref, sc_ref, shn_ref, scn_ref, gt_ref, w1_ref, w3_ref, w2_ref = refs[pos:pos + 9]
    pos += 9
    fn_ref = None
    if final:
        fn_ref = refs[pos]
        pos += 1
    o_ref, h_ref, res_ref, acc_ref = refs[pos:pos + 4]
    i = pl.program_id(0)
    j = pl.program_id(1)
    cur = i % 2
    g = g_ref[...]

    @pl.when((i == 0) & (j == 0))
    def _():
        x0 = x0_ref[...]
        res_ref[0] = x0
        h_ref[0] = _modulate(x0, g, sh_ref[...], sc_ref[...]).astype(BF16)

    h = h_ref[cur]
    a = _dot(h, w1_ref[...])
    b = _dot(h, w3_ref[...])
    act = ((a * jax.nn.sigmoid(a)) * b).astype(BF16)
    acc_ref[...] = jnp.where(j > 0, acc_ref[...], 0.0) + _dot(act, w2_ref[...])

    pro_rows = xn_ref.shape[0]
    r0 = pl.multiple_of(jnp.minimum(j, n_pro - 1) * pro_rows, pro_rows)
    nxt = xn_ref[...]
    if c_ref is not None:
        nxt = jnp.where(i + 1 < n_lat_tiles, nxt, c_ref[...])
    res_ref[1 - cur, pl.ds(r0, pro_rows), :] = nxt
    h_ref[1 - cur, pl.ds(r0, pro_rows), :] = _modulate(nxt, g, shn_ref[...], scn_ref[...]).astype(BF16)

    @pl.when(j == n_ffn_tiles - 1)
    def _():
        y = res_ref[cur] + (0.5 * gt_ref[...]) * acc_ref[...]
        if final:
            y = _rms(y) * fn_ref[...]
        o_ref[...] = y


def _ffn(x, n_rows, cidx, g, shift, scale, gate, w1, w3, w2, final_g=None, x_ctx=None):
    d = x.shape[1]
    nj = w1.shape[-1] // FFN_TILE
    n_tiles = n_rows // ROW_TILE
    final = final_g is not None
    n_pro = 1
    while n_pro * 2 <= min(nj, ROW_TILE // 16):
        n_pro *= 2
    pro_rows = ROW_TILE // n_pro
    n_lat_tiles = None
    if x_ctx is not None:
        assert x_ctx.shape[0] == ROW_TILE and x.shape[0] + ROW_TILE == n_rows
        n_lat_tiles = n_tiles - 1
    last_src = (n_lat_tiles if x_ctx is not None else n_tiles) - 1

    def nxt(i):
        return jnp.minimum(i + 1, n_tiles - 1)

    def chunk(j):
        return jnp.minimum(j, n_pro - 1)

    vec = pl.BlockSpec((None, 1, d), lambda i, j: (cidx(i), 0, 0))
    vec_next = pl.BlockSpec((None, 1, d), lambda i, j: (cidx(nxt(i)), 0, 0))
    const_row = pl.BlockSpec((1, d), lambda i, j: (0, 0))
    in_specs = [
        pl.BlockSpec((ROW_TILE, d), lambda i, j: (0, 0), pipeline_mode=pl.Buffered(1)),
        pl.BlockSpec((pro_rows, d), lambda i, j: (jnp.minimum(i + 1, last_src) * n_pro + chunk(j), 0)),
    ]
    args = [x, x]
    if x_ctx is not None:
        in_specs.append(pl.BlockSpec((pro_rows, d), lambda i, j: (chunk(j), 0)))
        args.append(x_ctx)
    in_specs += [
        const_row, vec, vec, vec_next, vec_next, vec,
        pl.BlockSpec((d, FFN_TILE), lambda i, j: (0, j)),
        pl.BlockSpec((d, FFN_TILE), lambda i, j: (0, j)),
        pl.BlockSpec((FFN_TILE, d), lambda i, j: (j, 0)),
    ]
    args += [g.reshape(1, d), shift, scale, shift, scale, gate, w1, w3, w2]
    if final:
        in_specs.append(const_row)
        args.append(final_g.reshape(1, d))
    return pl.pallas_call(
        functools.partial(_ffn_kernel, n_ffn_tiles=nj, n_pro=n_pro, n_lat_tiles=n_lat_tiles, final=final),
        grid=(n_tiles, nj),
        in_specs=in_specs,
        out_specs=pl.BlockSpec((ROW_TILE, d), lambda i, j: (i, 0)),
        out_shape=jax.ShapeDtypeStruct((n_rows, d), F32),
        scratch_shapes=[pltpu.VMEM((2, ROW_TILE, d), BF16), pltpu.VMEM((2, ROW_TILE, d), F32),
                        pltpu.VMEM((ROW_TILE, d), F32)],
        compiler_params=_cparams(("arbitrary", "arbitrary")),
        name="ffn_final" if final else ("ffn_split" if x_ctx is not None else "ffn"),
    )(*args)


def _ab_in_kernel(x_ref, g_ref, sh_ref, sc_ref, w_ref, cos_ref, sin_ref,
                  q_ref, kk_ref, vv_ref, u_ref, gv_ref):
    h = _modulate(x_ref[...], g_ref[...], sh_ref[...], sc_ref[...]).astype(BF16)
    cos = cos_ref[...]
    sin = sin_ref[...]
    q_scale = A_HEAD_DIM ** -0.5 * LOG2E
    zq = _dot(h, w_ref[:, 0:A_Q_W])
    for p in range(A_Q_W // LANES):
        slab = _rope(zq[:, p * LANES:(p + 1) * LANES], cos, sin)
        q_ref[p] = (slab * q_scale).astype(BF16)
    zkv = _dot(h, w_ref[:, A_Q_W:A_Q_W + 2 * A_KV_W])
    k = _rope(zkv[:, 0:LANES], cos, sin)
    v = zkv[:, LANES:2 * LANES]
    kk_ref[:, 0:LANES] = k.astype(BF16)
    kk_ref[:, LANES:2 * LANES] = pltpu.roll(k, A_HEAD_DIM, 1).astype(BF16)
    vv_ref[:, 0:LANES] = v.astype(BF16)
    vv_ref[:, LANES:2 * LANES] = pltpu.roll(v, A_HEAD_DIM, 1).astype(BF16)
    off = A_Q_W + 2 * A_KV_W
    u_ref[...] = _dot(h, w_ref[:, off:off + B_W])
    gv_ref[...] = jax.nn.gelu(_dot(h, w_ref[:, off + B_W:off + 2 * B_W])).astype(BF16)


def _ab_in(x, cidx, tab_idx, g, shift, scale, w_in, cos_t, sin_t, cast=None):
    n_rows, d = x.shape
    vec = pl.BlockSpec((None, 1, d), lambda i: (cidx(i), 0, 0))
    tab = pl.BlockSpec((ROW_TILE, LANES), lambda i: (tab_idx(i), 0))
    n_pairs = A_Q_W // LANES
    return _row_tile_call(
        _ab_in_kernel,
        n_rows // ROW_TILE,
        in_specs=[
            pl.BlockSpec((ROW_TILE, d), lambda i: (i, 0)),
            pl.BlockSpec((1, d), lambda i: (0, 0)),
            vec, vec,
            _resident(w_in.shape),
            tab, tab,
        ],
        out_specs=[
            pl.BlockSpec((n_pairs, ROW_TILE, LANES), lambda i: (0, i, 0)),
            pl.BlockSpec((ROW_TILE, 2 * LANES), lambda i: (i, 0)),
            pl.BlockSpec((ROW_TILE, 2 * LANES), lambda i: (i, 0)),
            pl.BlockSpec((ROW_TILE, B_W), lambda i: (i, 0)),
            pl.BlockSpec((ROW_TILE, B_W), lambda i: (i, 0)),
        ],
        out_shape=[
            jax.ShapeDtypeStruct((n_pairs, n_rows, LANES), BF16),
            jax.ShapeDtypeStruct((n_rows, 2 * LANES), BF16),
            jax.ShapeDtypeStruct((n_rows, 2 * LANES), BF16),
            jax.ShapeDtypeStruct((n_rows, B_W), F32),
            jax.ShapeDtypeStruct((n_rows, B_W), BF16),
        ],
        name="ab_in",
        args=(x, g.reshape(1, d), shift, scale, w_in, cos_t, sin_t),
        cast=cast,
    )


def _win_kernel(sink_ref, q_ref, k0_ref, k1_ref, k2_ref, k3_ref, kx_ref, v0_ref, v1_ref, v2_ref, v3_ref, vx_ref,
                o_ref, s_ref, *, n_lat_blocks, seq):
    step = pl.program_id(1)
    blk = ATT_BLOCK
    n_lat_keys = 3 * blk
    pairs_per_kv = (A_HEADS // A_KV_HEADS) // 2
    k_blocks = [k0_ref[...], k1_ref[...], k2_ref[...], k3_ref[...]]
    v_blocks = [v0_ref[...], v1_ref[...], v2_ref[...], v3_ref[...]]
    n_keys = n_lat_keys + kx_ref.shape[0]
    lo = lax.broadcasted_iota(jnp.int32, (n_keys, LANES), 1) < A_HEAD_DIM
    zero = jnp.zeros((n_keys, LANES), BF16)
    qo = lax.broadcasted_iota(jnp.int32, (blk, n_keys), 0)
    ko = lax.broadcasted_iota(jnp.int32, (blk, n_keys), 1)
    rel = ko - blk - qo
    in_window = jnp.where(jnp.abs(rel) <= WINDOW, 1, 0)
    is_ctx_key = jnp.where(ko >= n_lat_keys, 1, 0)

    kalls, valls, biases = [], [], []
    for qb in range(WIN_Q_BLOCKS):
        kalls.append(jnp.concatenate(k_blocks[qb:qb + 3] + [kx_ref[...]], axis=0))
        valls.append(jnp.concatenate(v_blocks[qb:qb + 3] + [vx_ref[...]], axis=0))
        i = step * WIN_Q_BLOCKS + qb
        kpos = i * blk - blk + ko
        in_range = jnp.where(kpos >= 0, 1, 0) * jnp.where(kpos < seq, 1, 0)
        latent_query = jnp.where(i < n_lat_blocks, 1, 0)
        valid = (in_window * in_range * latent_query + is_ctx_key) > 0
        bias = jnp.where(valid, 0.0, -1e30).astype(F32)
        biases.append(jnp.concatenate([bias] * pairs_per_kv, axis=0))

    units = [(qb, kv, par) for qb in range(WIN_Q_BLOCKS) for kv in range(A_KV_HEADS) for par in range(2)]

    def lane_half(x, kv, par):
        src = x[:, 0:LANES] if (kv + par) % 2 == 0 else x[:, LANES:2 * LANES]
        return jnp.where(lo, src, zero) if par == 0 else jnp.where(lo, zero, src)

    def scores(u):
        qb, kv, par = units[u]
        qg = q_ref[kv * pairs_per_kv:(kv + 1) * pairs_per_kv, qb * blk:(qb + 1) * blk, :]
        qg = qg.reshape(pairs_per_kv * blk, LANES)
        s_ref[u % 2] = _dot_nt(qg, lane_half(kalls[qb], kv, par))

    def finish(u):
        qb, kv, par = units[u]
        s = s_ref[u % 2] + biases[qb]
        sink_col = LOG2E * jnp.concatenate(
            [jnp.full((blk, 1), sink_ref[kv * 2 * pairs_per_kv + 2 * pp + par], F32)
             for pp in range(pairs_per_kv)], axis=0)
        m = jnp.maximum(jnp.max(s, axis=-1, keepdims=True), sink_col)
        p = jnp.exp2(s - m)
        denom = jnp.sum(p, axis=-1, keepdims=True) + jnp.exp2(sink_col - m)
        return _dot(p.astype(BF16), lane_half(valls[qb], kv, par)) / denom

    scores(0)
    acc = None
    for u, (qb, kv, par) in enumerate(units):
        if u + 1 < len(units):
            scores(u + 1)
        out = finish(u)
        acc = out if par == 0 else acc + out
        if par == 1:
            for pp in range(pairs_per_kv):
                col = (kv * pairs_per_kv + pp) * LANES
                o_ref[qb * blk:(qb + 1) * blk, col:col + LANES] = acc[pp * blk:(pp + 1) * blk].astype(BF16)


def _win_attention(q, kk, vv, sink, batch, seq, ctx_len):
    n_pairs, n_rows, _ = q.shape
    blk = ATT_BLOCK
    nq = WIN_Q_BLOCKS
    nlb = seq // blk
    ncb = ctx_len // blk
    assert nq == 2 and nlb % nq == 0 and ncb % nq == 0
    lat_steps = nlb // nq
    ctx_steps = ncb // nq

    def q_idx(b, i):
        return jnp.where(i < lat_steps, b * lat_steps + i, batch * lat_steps + b * ctx_steps + (i - lat_steps))

    def k_idx(b, i, j):
        return b * nlb + jnp.clip(i * nq - 1 + j, 0, nlb - 1)

    ctx_block0 = (batch * seq) // ctx_len
    two = 2 * LANES

    def nb(j):
        return pl.BlockSpec((blk, two), lambda b, i, s: (k_idx(b, i, j), 0))

    ctx_spec = pl.BlockSpec((ctx_len, two), lambda b, i, s: (ctx_block0 + b, 0))
    neighbours = [nb(j) for j in range(nq + 2)]
    grid_spec = pltpu.PrefetchScalarGridSpec(
        num_scalar_prefetch=1,
        grid=(batch, lat_steps + ctx_steps),
        in_specs=[pl.BlockSpec((n_pairs, nq * blk, LANES), lambda b, i, s: (0, q_idx(b, i), 0))]
        + neighbours + [ctx_spec] + neighbours + [ctx_spec],
        out_specs=pl.BlockSpec((nq * blk, A_Q_W), lambda b, i, s: (q_idx(b, i), 0)),
        scratch_shapes=[pltpu.VMEM((2, (n_pairs // A_KV_HEADS) * blk, 3 * blk + ctx_len), F32)],
    )
    return pl.pallas_call(
        functools.partial(_win_kernel, n_lat_blocks=nlb, seq=seq),
        grid_spec=grid_spec,
        out_shape=jax.ShapeDtypeStruct((n_rows, A_Q_W), BF16),
        compiler_params=_cparams(("arbitrary", "arbitrary")),
        name="win_attn",
    )(sink, q, *([kk] * (nq + 3)), *([vv] * (nq + 3)))


def _gmlp_kernel(u_ref, gv_ref, ws_ref, bias_ref, o_ref):
    n_chunks = u_ref.shape[0] // B_CHUNK
    for g in range(B_GROUPS):
        cols = slice(g * LANES, (g + 1) * LANES)
        rhs = jnp.concatenate(
            [gv_ref[c * B_CHUNK:(c + 1) * B_CHUNK, cols] for c in range(n_chunks)], axis=1)
        mixed = _dot(ws_ref[g], rhs) + bias_ref[:, g:g + 1]
        for c in range(n_chunks):
            rows = slice(c * B_CHUNK, (c + 1) * B_CHUNK)
            gu = jax.nn.gelu(u_ref[rows, cols])
            o_ref[rows, cols] = (gu * mixed[:, c * LANES:(c + 1) * LANES]).astype(BF16)


def _gmlp(u, gv, ws, bias_t):
    n_rows = u.shape[0]
    return pl.pallas_call(
        _gmlp_kernel,
        grid=(n_rows // ROW_TILE,),
        in_specs=[
            pl.BlockSpec((ROW_TILE, B_W), lambda i: (i, 0)),
            pl.BlockSpec((ROW_TILE, B_W), lambda i: (i, 0)),
            pl.BlockSpec(ws.shape, lambda i: (0, 0, 0)),
            pl.BlockSpec(bias_t.shape, lambda i: (0, 0)),
        ],
        out_specs=pl.BlockSpec((ROW_TILE, B_W), lambda i: (i, 0)),
        out_shape=jax.ShapeDtypeStruct((n_rows, B_W), BF16),
        compiler_params=_cparams(("arbitrary",)),
        name="gmlp",
    )(u, gv, ws, bias_t)


def _ab_out_kernel(x_ref, oa_ref, ob_ref, w_ref, gt_ref, o_ref):
    half = oa_ref.shape[1]
    y = _dot(oa_ref[...], w_ref[0:half, :]) + _dot(ob_ref[...], w_ref[half:2 * half, :])
    o_ref[...] = x_ref[...] + gt_ref[...] * y


def _ab_out(x, oa, ob, w_out, cidx, gate, cast=None):
    n_rows, d = x.shape
    half = oa.shape[1]
    (y,), wb = _row_tile_call(
        _ab_out_kernel,
        n_rows // ROW_TILE,
        in_specs=[
            pl.BlockSpec((ROW_TILE, d), lambda i: (i, 0)),
            pl.BlockSpec((ROW_TILE, half), lambda i: (i, 0)),
            pl.BlockSpec((ROW_TILE, half), lambda i: (i, 0)),
            _resident(w_out.shape),
            pl.BlockSpec((None, 1, d), lambda i: (cidx(i), 0, 0)),
        ],
        out_specs=[pl.BlockSpec((ROW_TILE, d), lambda i: (i, 0))],
        out_shape=[jax.ShapeDtypeStruct((n_rows, d), F32)],
        name="ab_out",
        args=(x, oa, ob, w_out, gate),
        cast=cast,
    )
    return y, wb


def _cd_in_kernel(x_ref, g_ref, sh_ref, sc_ref, w_ref, qn_ref, kvn_ref, cos_ref, sin_ref,
                  cq_ref, ckv_ref, kr_ref, db_ref, t_ref):
    h = _modulate(x_ref[...], g_ref[...], sh_ref[...], sc_ref[...]).astype(BF16)
    o0 = C_Q_RANK
    o1 = o0 + C_KV_RANK
    o2 = o1 + D_W
    o3 = o2 + D_W
    o4 = o3 + D_W
    cq_ref[...] = (_rms(_dot(h, w_ref[:, 0:o0])) * qn_ref[...]).astype(BF16)
    ckv_ref[...] = (_rms(_dot(h, w_ref[:, o0:o1])) * kvn_ref[...]).astype(BF16)
    db_ref[...] = _dot(h, w_ref[:, o1:o2])
    t_ref[...] = _dot(h, w_ref[:, o2:o3]) * _dot(h, w_ref[:, o3:o4])
    kr_ref[...] = _rope(_dot(h, w_ref[:, o4:o4 + LANES]), cos_ref[...], sin_ref[...]).astype(BF16)


def _cd_in(x, cidx, tab_idx, g, shift, scale, w_in, q_norm, kv_norm, cos_t, sin_t):
    n_rows, d = x.shape
    vec = pl.BlockSpec((None, 1, d), lambda i: (cidx(i), 0, 0))
    tab = pl.BlockSpec((ROW_TILE, LANES), lambda i: (tab_idx(i), 0))

    def rows(width):
        return pl.BlockSpec((ROW_TILE, width), lambda i: (i, 0))

    return pl.pallas_call(
        _cd_in_kernel,
        grid=(n_rows // ROW_TILE,),
        in_specs=[
            rows(d),
            pl.BlockSpec((1, d), lambda i: (0, 0)),
            vec, vec,
            _resident(w_in.shape),
            pl.BlockSpec((1, C_Q_RANK), lambda i: (0, 0)),
            pl.BlockSpec((1, C_KV_RANK), lambda i: (0, 0)),
            tab, tab,
        ],
        out_specs=[rows(C_Q_RANK), rows(C_KV_RANK), rows(LANES), rows(D_W), rows(D_W)],
        out_shape=[
            jax.ShapeDtypeStruct((n_rows, C_Q_RANK), BF16),
            jax.ShapeDtypeStruct((n_rows, C_KV_RANK), BF16),
            jax.ShapeDtypeStruct((n_rows, LANES), BF16),
            jax.ShapeDtypeStruct((n_rows, D_W), F32),
            jax.ShapeDtypeStruct((n_rows, D_W), F32),
        ],
        compiler_params=_cparams(("arbitrary",)),
        name="cd_in",
    )(x, g.reshape(1, d), shift, scale, w_in, q_norm.reshape(1, -1), kv_norm.reshape(1, -1),
      cos_t, sin_t)


def _mla_q_kernel(cq_ref, wn_ref, wr_ref, cos_ref, sin_ref, q_ref):
    cq = cq_ref[...]
    scale = (C_NOPE + C_ROPE) ** -0.5 * LOG2E
    qn = _dot(cq, wn_ref[...]) * scale
    qr = _dot(cq, wr_ref[...]) * scale
    cos = cos_ref[...]
    sin = sin_ref[...]
    lo = lax.broadcasted_iota(jnp.int32, (cq.shape[0], LANES), 1) < C_ROPE
    for pair in range(C_HEADS // 2):
        slab = _rope(qr[:, pair * LANES:(pair + 1) * LANES], cos, sin)
        for par in range(2):
            hd = 2 * pair + par
            q_ref[:, 2 * hd * LANES:(2 * hd + 1) * LANES] = (
                qn[:, hd * LANES:(hd + 1) * LANES].astype(BF16))
            rot = jnp.where(lo, slab, 0.0) if par == 0 else jnp.where(lo, 0.0, slab)
            q_ref[:, (2 * hd + 1) * LANES:(2 * hd + 2) * LANES] = rot.astype(BF16)


def _mla_q(cq, n_rows, w_qn, w_qr, cos_t, sin_t, tab_idx):
    width = 2 * LANES * C_HEADS
    tab = pl.BlockSpec((ROW_TILE, LANES), lambda i: (tab_idx(i), 0))
    return pl.pallas_call(
        _mla_q_kernel,
        grid=(n_rows // ROW_TILE,),
        in_specs=[
            pl.BlockSpec((ROW_TILE, C_Q_RANK), lambda i: (i, 0)),
            _resident(w_qn.shape), _resident(w_qr.shape),
            tab, tab,
        ],
        out_specs=pl.BlockSpec((ROW_TILE, width), lambda i: (i, 0)),
        out_shape=jax.ShapeDtypeStruct((n_rows, width), BF16),
        compiler_params=_cparams(("arbitrary",)),
        name="mla_q",
    )(cq, w_qn, w_qr, cos_t, sin_t)


def _mla_kv_kernel(ckv_ref, kr_ref, wk_ref, wv_ref, k_ref, v_ref):
    ckv = ckv_ref[...]
    kn = _dot(ckv, wk_ref[...])
    kr = kr_ref[...]
    for hd in range(C_HEADS):
        k_ref[:, 2 * hd * LANES:(2 * hd + 1) * LANES] = kn[:, hd * LANES:(hd + 1) * LANES].astype(BF16)
        k_ref[:, (2 * hd + 1) * LANES:(2 * hd + 2) * LANES] = kr
    v_ref[...] = _dot(ckv, wv_ref[...]).astype(BF16)


def _mla_kv(ckv, kr2, w_kn, w_v, batch, seq, ctx_len):
    n_rows = ckv.shape[0]
    tile = ctx_len
    tiles_per_seq = seq // tile
    n_lat_tiles = batch * tiles_per_seq
    kw = 2 * LANES * C_HEADS
    vw = C_V * C_HEADS

    def out_idx(j):
        is_lat = j < n_lat_tiles
        b = jnp.where(is_lat, j // tiles_per_seq, j - n_lat_tiles)
        r = jnp.where(is_lat, j % tiles_per_seq, tiles_per_seq)
        return b, r, 0

    return pl.pallas_call(
        _mla_kv_kernel,
        grid=(n_rows // tile,),
        in_specs=[
            pl.BlockSpec((tile, C_KV_RANK), lambda j: (j, 0)),
            pl.BlockSpec((tile, LANES), lambda j: (j, 0)),
            _resident(w_kn.shape), _resident(w_v.shape),
        ],
        out_specs=[pl.BlockSpec((None, tile, kw), out_idx),
                   pl.BlockSpec((None, tile, vw), out_idx)],
        out_shape=[jax.ShapeDtypeStruct((batch, seq + ctx_len, kw), BF16),
                   jax.ShapeDtypeStruct((batch, seq + ctx_len, vw), BF16)],
        compiler_params=_cparams(("arbitrary",)),
        name="mla_kv",
    )(ckv, kr2, w_kn, w_v)


def _mla_kernel(q_ref, k_ref, v_ref, o_ref, s_ref, m_ref, l_ref, acc_ref, *, tk, n_chunks):
    q = q_ref[...]
    reps = tk // LANES

    def rows(c):
        start = c * tk
        if not isinstance(start, int):
            start = pl.multiple_of(start, tk)
        return pl.ds(start, tk)

    def scores(c, slot):
        s_ref[slot] = _dot_nt(q, k_ref[rows(c), :])

    def absorb(c, slot):
        v = v_ref[rows(c), :]
        for rb in range(q.shape[0] // MLA_ROW_BLOCK):
            blk = slice(rb * MLA_ROW_BLOCK, (rb + 1) * MLA_ROW_BLOCK)
            s = s_ref[slot, blk, :]
            m_old = m_ref[blk, :]
            m_new = jnp.maximum(m_old, jnp.broadcast_to(jnp.max(s, axis=-1, keepdims=True), m_old.shape))
            alpha = jnp.exp2(m_old - m_new)
            p = jnp.exp2(s - jnp.concatenate([m_new] * reps, axis=1))
            l_ref[blk, :] = alpha * l_ref[blk, :] + jnp.broadcast_to(
                jnp.sum(p, axis=-1, keepdims=True), m_old.shape)
            acc_ref[blk, :] = alpha * acc_ref[blk, :] + _dot(p.astype(BF16), v)
            m_ref[blk, :] = m_new

    m_ref[...] = jnp.full(m_ref.shape, -1e30, F32)
    l_ref[...] = jnp.zeros(l_ref.shape, F32)
    acc_ref[...] = jnp.zeros(acc_ref.shape, F32)
    scores(0, 0)
    n_pairs = (n_chunks - 1) // 2

    def pair(i, carry):
        scores(2 * i + 1, 1)
        absorb(2 * i, 0)
        scores(2 * i + 2, 0)
        absorb(2 * i + 1, 1)
        return carry

    for i in range(n_pairs):
        pair(i, 0)
    done = 2 * n_pairs
    if n_chunks - done == 2:
        scores(done + 1, 1)
        absorb(done, 0)
        absorb(done + 1, 1)
    else:
        absorb(done, 0)
    o_ref[...] = (acc_ref[...] / l_ref[...]).astype(BF16)


def _mla_key_tile(n_keys):
    best = LANES
    for t in range(LANES, MLA_K_TILE_MAX + 1, LANES):
        if n_keys % t == 0:
            best = t
    return best


def _mla_attention(q, k, v, batch, seq):
    n_lat = batch * seq
    n_keys = k.shape[1]
    tq = MLA_Q_TILE
    nq = seq // tq
    tk = _mla_key_tile(n_keys)
    kw = 2 * LANES
    return pl.pallas_call(
        functools.partial(_mla_kernel, tk=tk, n_chunks=n_keys // tk),
        grid=(batch, C_HEADS, nq),
        in_specs=[
            pl.BlockSpec((tq, kw), lambda b, h, i: (b * nq + i, h)),
            pl.BlockSpec((None, n_keys, kw), lambda b, h, i: (b, 0, h)),
            pl.BlockSpec((None, n_keys, C_V), lambda b, h, i: (b, 0, h)),
        ],
        out_specs=pl.BlockSpec((tq, C_V), lambda b, h, i: (b * nq + i, h)),
        out_shape=jax.ShapeDtypeStruct((n_lat, C_V * C_HEADS), BF16),
        scratch_shapes=[pltpu.VMEM((2, tq, tk), F32),
                        pltpu.VMEM((tq, LANES), F32),
                        pltpu.VMEM((tq, LANES), F32),
                        pltpu.VMEM((tq, C_V), F32)],
        compiler_params=_cparams(("arbitrary", "arbitrary", "arbitrary")),
        name="mla_attn",
    )(q, k, v)


def _cd_out_kernel(x_ref, oc_ref, db_ref, t_ref, tp_ref, tn_ref, cw_ref, w_ref, gt_ref, o_ref,
                   *, tiles_per_seq):
    i = pl.program_id(0)
    t = t_ref[...]
    rows = t.shape[0]
    row = lax.broadcasted_iota(jnp.int32, t.shape, 0)
    pos = i % tiles_per_seq
    above = jnp.where(pos > 0, tp_ref[SUBLANES - 1:SUBLANES, :], 0.0)
    below = jnp.where(pos < tiles_per_seq - 1, tn_ref[0:1, :], 0.0)
    t_up = jnp.where(row == 0, above, pltpu.roll(t, 1, 0))
    t_dn = jnp.where(row == rows - 1, below, pltpu.roll(t, rows - 1, 0))
    conv = cw_ref[0:1, :] * t_up + cw_ref[1:2, :] * t + cw_ref[2:3, :] * t_dn
    od = (db_ref[...] * conv).astype(BF16)
    half = oc_ref.shape[1]
    y = _dot(oc_ref[...], w_ref[0:half, :]) + _dot(od, w_ref[half:2 * half, :])
    o_ref[...] = x_ref[...] + gt_ref[...] * y


def _cd_out(x, n_rows, oc, db, t, conv_w, w_out, cidx, gate, tiles_per_seq, cast=None):
    d = x.shape[1]
    halo_per_tile = ROW_TILE // SUBLANES
    n_halo = t.shape[0] // SUBLANES
    (y,), wb = _row_tile_call(
        functools.partial(_cd_out_kernel, tiles_per_seq=tiles_per_seq),
        n_rows // ROW_TILE,
        in_specs=[
            pl.BlockSpec((ROW_TILE, d), lambda i: (i, 0)),
            pl.BlockSpec((ROW_TILE, D_W), lambda i: (i, 0)),
            pl.BlockSpec((ROW_TILE, D_W), lambda i: (i, 0)),
            pl.BlockSpec((ROW_TILE, D_W), lambda i: (i, 0)),
            pl.BlockSpec((SUBLANES, D_W), lambda i: (jnp.maximum(i * halo_per_tile - 1, 0), 0)),
            pl.BlockSpec((SUBLANES, D_W),
                         lambda i: (jnp.minimum((i + 1) * halo_per_tile, n_halo - 1), 0)),
            pl.BlockSpec(conv_w.shape, lambda i: (0, 0)),
            _resident(w_out.shape),
            pl.BlockSpec((None, 1, d), lambda i: (cidx(i), 0, 0)),
        ],
        out_specs=[pl.BlockSpec((ROW_TILE, d), lambda i: (i, 0))],
        out_shape=[jax.ShapeDtypeStruct((n_rows, d), F32)],
        name="cd_out",
        args=(x, oc, db, t, t, t, conv_w, w_out, gate),
        cast=cast,
    )
    return y, wb


def _rope_tables(seq, n_ctx_rows):
    t = jnp.arange(seq)
    row = (t // GRID_W).astype(F32)
    col = (t % GRID_W).astype(F32)
    axis_dim = ROPE_DIM // 2
    inv_freq = ROPE_BASE ** (-jnp.arange(0, axis_dim, 2, dtype=F32) / axis_dim)
    ang_r = row[:, None] * inv_freq
    ang_c = col[:, None] * inv_freq
    cos_h = jnp.concatenate([jnp.cos(ang_r)] * 2 + [jnp.cos(ang_c)] * 2, axis=-1)
    sin_h = jnp.concatenate([-jnp.sin(ang_r), jnp.sin(ang_r), -jnp.sin(ang_c), jnp.sin(ang_c)], axis=-1)
    reps = LANES // ROPE_DIM
    cos_t = jnp.concatenate([jnp.tile(cos_h, (1, reps)), jnp.ones((n_ctx_rows, LANES), F32)], axis=0)
    sin_t = jnp.concatenate([jnp.tile(sin_h, (1, reps)), jnp.zeros((n_ctx_rows, LANES), F32)], axis=0)
    return cos_t, sin_t


def kernel(x, c, ctx, c_ctx, mod_w, mod_b, norm_g, ffn_w1, ffn_w3, ffn_w2, ab_w_in, ab_w_out, a_sink,
           b_ws, b_bias, cd_w_in, cd_w_out, c_q_norm, c_kv_norm, c_w_uq, c_w_ukv, d_conv_w, final_norm):
    batch, seq, d = x.shape
    ctx_len = ctx.shape[1]
    depth = mod_w.shape[0]
    n_lat = batch * seq
    n_ctx = batch * ctx_len
    n_all = n_lat + n_ctx
    assert depth == 2, "layer 0 mixes with A||B, layer 1 (last) with C||D"
    assert seq % ROW_TILE == 0 and n_ctx % ROW_TILE == 0 and ROW_TILE % ctx_len == 0
    assert seq % GRID_W == 0 and seq % MLA_Q_TILE == 0 and seq % ctx_len == 0
    assert ctx_len % ATT_BLOCK == 0 and n_lat % ctx_len == 0
    assert (N_MOD * d) % MOD_TILE == 0 and ffn_w1.shape[-1] % FFN_TILE == 0 and d % LANES == 0

    tiles_per_seq = seq // ROW_TILE
    n_lat_tiles = n_lat // ROW_TILE

    def cidx(i):
        return jnp.minimum(i // tiles_per_seq, batch)

    def tab_idx(i):
        return jnp.where(i < n_lat_tiles, i % tiles_per_seq, tiles_per_seq + i - n_lat_tiles)

    cond = jnp.zeros((SUBLANES, d), F32).at[:batch].set(c).at[batch].set(c_ctx)
    mod = _modvec(cond, mod_w, mod_b)
    mod = mod[:, :batch + 1].reshape(depth, batch + 1, N_MOD, 1, d)

    def mvec(layer, k):
        return mod[layer, :, k]

    cos_t, sin_t = _rope_tables(seq, n_ctx)

    wb0 = (ffn_w1[0, 0].astype(BF16), ffn_w3[0, 0].astype(BF16), ffn_w2[0, 0].astype(BF16))

    def ffn_cast(layer, k):
        return (ffn_w1, ffn_w3, ffn_w2), (layer, k)

    xs = _ffn(x.reshape(n_lat, d), n_all, cidx, norm_g[0, 0], mvec(0, 0), mvec(0, 1), mvec(0, 2), *wb0,
              x_ctx=ctx.reshape(n_ctx, d))
    (q, kk, vv, u, gv), wb1 = _ab_in(xs, cidx, tab_idx, norm_g[0, 1], mvec(0, 3), mvec(0, 4),
                                     ab_w_in[0].astype(BF16), cos_t, sin_t, cast=ffn_cast(0, 1))
    oa = _win_attention(q, kk, vv, a_sink[0], batch, seq, ctx_len)
    ob = _gmlp(u, gv, b_ws[0].astype(BF16), b_bias[0].T)
    xs, wb2 = _ab_out(xs, oa, ob, ab_w_out[0].astype(BF16), cidx, mvec(0, 5), cast=ffn_cast(1, 0))
    xs = _ffn(xs, n_all, cidx, norm_g[0, 2], mvec(0, 6), mvec(0, 7), mvec(0, 8), *wb1)

    xs = _ffn(xs, n_all, cidx, norm_g[1, 0], mvec(1, 0), mvec(1, 1), mvec(1, 2), *wb2)
    wi = cd_w_in[0]
    o_kr = C_Q_RANK + C_KV_RANK
    w_kr = wi[:, o_kr:o_kr + C_ROPE]
    w_cd = jnp.concatenate([wi[:, :o_kr], wi[:, o_kr + C_ROPE:], w_kr, w_kr], axis=1).astype(BF16)
    cq, ckv, kr2, db, t = _cd_in(xs, cidx, tab_idx, norm_g[1, 1], mvec(1, 3), mvec(1, 4), w_cd,
                                 c_q_norm[0], c_kv_norm[0], cos_t, sin_t)
    w_uq = c_w_uq[0].reshape(C_Q_RANK, C_HEADS, C_NOPE + C_ROPE)
    w_qn = w_uq[:, :, :C_NOPE].reshape(C_Q_RANK, C_HEADS * C_NOPE).astype(BF16)
    w_qr = w_uq[:, :, C_NOPE:].reshape(C_Q_RANK, C_HEADS * C_ROPE).astype(BF16)
    w_ukv = c_w_ukv[0].reshape(C_KV_RANK, C_HEADS, C_NOPE + C_V)
    w_kn = w_ukv[:, :, :C_NOPE].reshape(C_KV_RANK, C_HEADS * C_NOPE).astype(BF16)
    w_v = w_ukv[:, :, C_NOPE:].reshape(C_KV_RANK, C_HEADS * C_V).astype(BF16)
    qm = _mla_q(cq, n_lat, w_qn, w_qr, cos_t, sin_t, tab_idx)
    km, vm = _mla_kv(ckv, kr2, w_kn, w_v, batch, seq, ctx_len)
    oc = _mla_attention(qm, km, vm, batch, seq)
    xl, wb3 = _cd_out(xs, n_lat, oc, db, t, d_conv_w[0], cd_w_out[0].astype(BF16), cidx, mvec(1, 5),
                      tiles_per_seq, cast=ffn_cast(1, 1))
    out = _ffn(xl, n_lat, cidx, norm_g[1, 2], mvec(1, 6), mvec(1, 7), mvec(1, 8), *wb3,
               final_g=final_norm)
    return out.reshape(batch, seq, d)
```

```python
import functools

import jax
import jax.numpy as jnp
from jax import lax
from jax.experimental import pallas as pl
from jax.experimental.pallas import tpu as pltpu

F32 = jnp.float32
BF16 = jnp.bfloat16

GRID_W = 64
ROPE_BASE = 10000.0
ROPE_DIM = 64
EPS = 1e-6
LOG2E = 1.4426950408889634
N_MOD = 9
WINDOW = 128
A_HEADS = 16
A_KV_HEADS = 2
A_HEAD_DIM = 64
A_Q_W = A_HEADS * A_HEAD_DIM
A_KV_W = A_KV_HEADS * A_HEAD_DIM
B_GROUPS = 8
B_CHUNK = 128
B_W = 1024
C_HEADS = 8
C_NOPE = 128
C_ROPE = ROPE_DIM
C_V = 128
C_Q_RANK = 768
C_KV_RANK = 512
D_W = 1024

LANES = 128
SUBLANES = 8
ROW_TILE = 512
FFN_TILE = 512
ATT_BLOCK = 128
WIN_Q_BLOCKS = 2
MLA_Q_TILE = 1024
MLA_K_TILE_MAX = 768
MLA_ROW_BLOCK = 256
MOD_TILE = 1024
VMEM_LIMIT = 56 * 1024 * 1024


def _cparams(sem):
    return pltpu.CompilerParams(dimension_semantics=sem, vmem_limit_bytes=VMEM_LIMIT)


def _resident(shape):
    nd = len(shape)
    return pl.BlockSpec(shape, lambda *_: (0,) * nd, pipeline_mode=pl.Buffered(1))


def _rms(x):
    return x * lax.rsqrt(jnp.mean(x * x, axis=-1, keepdims=True) + EPS)


def _modulate(x, g, shift, scale):
    return (_rms(x) * g) * (1.0 + scale) + shift


def _rope(z, cos, sin):
    lane = lax.broadcasted_iota(jnp.int32, z.shape, 1)
    first = (lane % 32) < 16
    partner = jnp.where(first, pltpu.roll(z, LANES - 16, 1), pltpu.roll(z, 16, 1))
    return z * cos + partner * sin


def _dot(a, b):
    return jnp.dot(a, b, preferred_element_type=F32)


def _dot_nt(a, b):
    return lax.dot_general(a, b, (((1,), (1,)), ((), ())), preferred_element_type=F32)


def _with_weight_cast(kernel_fn, n_in, n_out):
    def wrapped(*refs):
        core_in = refs[:n_in]
        cast_in = refs[n_in:n_in + 3]
        core_out = refs[n_in + 3:n_in + 3 + n_out]
        cast_out = refs[n_in + 3 + n_out:n_in + 6 + n_out]
        kernel_fn(*core_in, *core_out, *refs[n_in + 6 + n_out:])
        for src, dst in zip(cast_in, cast_out):
            dst[...] = src[...].astype(BF16)
    return wrapped


def _weight_cast_specs(stacked, which, n_steps, lin):
    layer, k = which
    in_specs, out_specs, out_shapes = [], [], []
    for w in stacked:
        rows, cols = w.shape[-2:]
        n_slabs = 1
        while n_slabs * 2 <= n_steps and rows % (n_slabs * 2 * 16) == 0:
            n_slabs *= 2
        repeat = n_steps // n_slabs
        n_blocks = -(-n_steps // repeat)
        slab_rows = rows // n_slabs

        def in_map(*idx, repeat=repeat, n_slabs=n_slabs):
            return layer, k, jnp.minimum(lin(*idx) // repeat, n_slabs - 1), 0

        def out_map(*idx, repeat=repeat):
            return lin(*idx) // repeat, 0

        in_specs.append(pl.BlockSpec((None, None, slab_rows, cols), in_map))
        out_specs.append(pl.BlockSpec((slab_rows, cols), out_map))
        out_shapes.append(jax.ShapeDtypeStruct((n_blocks * slab_rows, cols), BF16))
    return in_specs, out_specs, out_shapes


def _row_tile_call(kernel_fn, n_tiles, in_specs, args, out_specs, out_shape, name, cast=None):
    if cast is None:
        outs = pl.pallas_call(kernel_fn, grid=(n_tiles,), in_specs=in_specs, out_specs=out_specs,
                              out_shape=out_shape, compiler_params=_cparams(("arbitrary",)), name=name)(*args)
        return list(outs), None
    stacked, which = cast
    c_in, c_out, c_shape = _weight_cast_specs(stacked, which, n_tiles, lambda i: i)
    outs = pl.pallas_call(
        _with_weight_cast(kernel_fn, len(in_specs), len(out_specs)),
        grid=(n_tiles,),
        in_specs=list(in_specs) + c_in,
        out_specs=list(out_specs) + c_out,
        out_shape=list(out_shape) + c_shape,
        compiler_params=_cparams(("arbitrary",)),
        name=name,
    )(*args, *stacked)
    n = len(out_specs)
    return list(outs[:n]), tuple(outs[n:])


def _modvec_kernel(c_ref, w_ref, b_ref, o_ref):
    c = c_ref[...]
    s = c * jax.nn.sigmoid(c)
    o_ref[...] = _dot(s.astype(BF16), w_ref[...].astype(BF16)) + b_ref[...]


def _modvec(cond, mod_w, mod_b):
    depth, d, n = mod_w.shape
    rows = cond.shape[0]
    return pl.pallas_call(
        _modvec_kernel,
        grid=(depth, n // MOD_TILE),
        in_specs=[
            pl.BlockSpec((rows, d), lambda l, j: (0, 0)),
            pl.BlockSpec((None, d, MOD_TILE), lambda l, j: (l, 0, j)),
            pl.BlockSpec((None, 1, MOD_TILE), lambda l, j: (l, 0, j)),
        ],
        out_specs=pl.BlockSpec((None, rows, MOD_TILE), lambda l, j: (l, 0, j)),
        out_shape=jax.ShapeDtypeStruct((depth, rows, n), F32),
        compiler_params=_cparams(("arbitrary", "arbitrary")),
        name="modvec",
    )(cond, mod_w, mod_b.reshape(depth, 1, n))


def _ffn_kernel(*refs, n_ffn_tiles, n_pro, n_lat_tiles, final):
    refs = list(refs)
    x0_ref, xn_ref = refs[0:2]
    pos = 2
    c_ref = None
    if n_lat_tiles is not None:
        c_ref = refs[pos]
        pos += 1
    g_ref, sh_ref, sc_ref, shn_ref, scn_ref, gt_ref, w1_ref, w3_ref, w2_ref = refs[pos:pos + 9]
    pos += 9
    fn_ref = None
    if final:
        fn_ref = refs[pos]
        pos += 1
    o_ref, h_ref, res_ref, acc_ref = refs[pos:pos + 4]
    i = pl.program_id(0)
    j = pl.program_id(1)
    cur = i % 2
    g = g_ref[...]

    @pl.when((i == 0) & (j == 0))
    def _():
        x0 = x0_ref[...]
        res_ref[0] = x0
        h_ref[0] = _modulate(x0, g, sh_ref[...], sc_ref[...]).astype(BF16)

    h = h_ref[cur]
    a = _dot(h, w1_ref[...])
    b = _dot(h, w3_ref[...])
    act = ((a * jax.nn.sigmoid(a)) * b).astype(BF16)
    acc_ref[...] = jnp.where(j > 0, acc_ref[...], 0.0) + _dot(act, w2_ref[...])

    pro_rows = xn_ref.shape[0]
    r0 = pl.multiple_of(jnp.minimum(j, n_pro - 1) * pro_rows, pro_rows)
    nxt = xn_ref[...]
    if c_ref is not None:
        nxt = jnp.where(i + 1 < n_lat_tiles, nxt, c_ref[...])
    res_ref[1 - cur, pl.ds(r0, pro_rows), :] = nxt
    h_ref[1 - cur, pl.ds(r0, pro_rows), :] = _modulate(nxt, g, shn_ref[...], scn_ref[...]).astype(BF16)

    @pl.when(j == n_ffn_tiles - 1)
    def _():
        y = res_ref[cur] + (0.5 * gt_ref[...]) * acc_ref[...]
        if final:
            y = _rms(y) * fn_ref[...]
        o_ref[...] = y


def _ffn(x, n_rows, cidx, g, shift, scale, gate, w1, w3, w2, final_g=None, x_ctx=None):
    d = x.shape[1]
    nj = w1.shape[-1] // FFN_TILE
    n_tiles = n_rows // ROW_TILE
    final = final_g is not None
    n_pro = 1
    while n_pro * 2 <= min(nj, ROW_TILE // 16):
        n_pro *= 2
    pro_rows = ROW_TILE // n_pro
    n_lat_tiles = None
    if x_ctx is not None:
        assert x_ctx.shape[0] == ROW_TILE and x.shape[0] + ROW_TILE == n_rows
        n_lat_tiles = n_tiles - 1
    last_src = (n_lat_tiles if x_ctx is not None else n_tiles) - 1

    def nxt(i):
        return jnp.minimum(i + 1, n_tiles - 1)

    def chunk(j):
        return jnp.minimum(j, n_pro - 1)

    vec = pl.BlockSpec((None, 1, d), lambda i, j: (cidx(i), 0, 0))
    vec_next = pl.BlockSpec((None, 1, d), lambda i, j: (cidx(nxt(i)), 0, 0))
    const_row = pl.BlockSpec((1, d), lambda i, j: (0, 0))
    in_specs = [
        pl.BlockSpec((ROW_TILE, d), lambda i, j: (0, 0), pipeline_mode=pl.Buffered(1)),
        pl.BlockSpec((pro_rows, d), lambda i, j: (jnp.minimum(i + 1, last_src) * n_pro + chunk(j), 0)),
    ]
    args = [x, x]
    if x_ctx is not None:
        in_specs.append(pl.BlockSpec((pro_rows, d), lambda i, j: (chunk(j), 0)))
        args.append(x_ctx)
    in_specs += [
        const_row, vec, vec, vec_next, vec_next, vec,
        pl.BlockSpec((d, FFN_TILE), lambda i, j: (0, j)),
        pl.BlockSpec((d, FFN_TILE), lambda i, j: (0, j)),
        pl.BlockSpec((FFN_TILE, d), lambda i, j: (j, 0)),
    ]
    args += [g.reshape(1, d), shift, scale, shift, scale, gate, w1, w3, w2]
    if final:
        in_specs.append(const_row)
        args.append(final_g.reshape(1, d))
    return pl.pallas_call(
        functools.partial(_ffn_kernel, n_ffn_tiles=nj, n_pro=n_pro, n_lat_tiles=n_lat_tiles, final=final),
        grid=(n_tiles, nj),
        in_specs=in_specs,
        out_specs=pl.BlockSpec((ROW_TILE, d), lambda i, j: (i, 0)),
        out_shape=jax.ShapeDtypeStruct((n_rows, d), F32),
        scratch_shapes=[pltpu.VMEM((2, ROW_TILE, d), BF16), pltpu.VMEM((2, ROW_TILE, d), F32),
                        pltpu.VMEM((ROW_TILE, d), F32)],
        compiler_params=_cparams(("arbitrary", "arbitrary")),
        name="ffn_final" if final else ("ffn_split" if x_ctx is not None else "ffn"),
    )(*args)


def _ab_in_kernel(x_ref, g_ref, sh_ref, sc_ref, w_ref, cos_ref, sin_ref,
                  q_ref, kk_ref, vv_ref, u_ref, gv_ref):
    h = _modulate(x_ref[...], g_ref[...], sh_ref[...], sc_ref[...]).astype(BF16)
    cos = cos_ref[...]
    sin = sin_ref[...]
    q_scale = A_HEAD_DIM ** -0.5 * LOG2E
    zq = _dot(h, w_ref[:, 0:A_Q_W])
    for p in range(A_Q_W // LANES):
        slab = _rope(zq[:, p * LANES:(p + 1) * LANES], cos, sin)
        q_ref[p] = (slab * q_scale).astype(BF16)
    zkv = _dot(h, w_ref[:, A_Q_W:A_Q_W + 2 * A_KV_W])
    k = _rope(zkv[:, 0:LANES], cos, sin)
    v = zkv[:, LANES:2 * LANES]
    kk_ref[:, 0:LANES] = k.astype(BF16)
    kk_ref[:, LANES:2 * LANES] = pltpu.roll(k, A_HEAD_DIM, 1).astype(BF16)
    vv_ref[:, 0:LANES] = v.astype(BF16)
    vv_ref[:, LANES:2 * LANES] = pltpu.roll(v, A_HEAD_DIM, 1).astype(BF16)
    off = A_Q_W + 2 * A_KV_W
    u_ref[...] = _dot(h, w_ref[:, off:off + B_W])
    gv_ref[...] = jax.nn.gelu(_dot(h, w_ref[:, off + B_W:off + 2 * B_W])).astype(BF16)


def _ab_in(x, cidx, tab_idx, g, shift, scale, w_in, cos_t, sin_t, cast=None):
    n_rows, d = x.shape
    vec = pl.BlockSpec((None, 1, d), lambda i: (cidx(i), 0, 0))
    tab = pl.BlockSpec((ROW_TILE, LANES), lambda i: (tab_idx(i), 0))
    n_pairs = A_Q_W // LANES
    return _row_tile_call(
        _ab_in_kernel,
        n_rows // ROW_TILE,
        in_specs=[
            pl.BlockSpec((ROW_TILE, d), lambda i: (i, 0)),
            pl.BlockSpec((1, d), lambda i: (0, 0)),
            vec, vec,
            _resident(w_in.shape),
            tab, tab,
        ],
        out_specs=[
            pl.BlockSpec((n_pairs, ROW_TILE, LANES), lambda i: (0, i, 0)),
            pl.BlockSpec((ROW_TILE, 2 * LANES), lambda i: (i, 0)),
            pl.BlockSpec((ROW_TILE, 2 * LANES), lambda i: (i, 0)),
            pl.BlockSpec((ROW_TILE, B_W), lambda i: (i, 0)),
            pl.BlockSpec((ROW_TILE, B_W), lambda i: (i, 0)),
        ],
        out_shape=[
            jax.ShapeDtypeStruct((n_pairs, n_rows, LANES), BF16),
            jax.ShapeDtypeStruct((n_rows, 2 * LANES), BF16),
            jax.ShapeDtypeStruct((n_rows, 2 * LANES), BF16),
            jax.ShapeDtypeStruct((n_rows, B_W), F32),
            jax.ShapeDtypeStruct((n_rows, B_W), BF16),
        ],
        name="ab_in",
        args=(x, g.reshape(1, d), shift, scale, w_in, cos_t, sin_t),
        cast=cast,
    )


def _win_kernel(sink_ref, q_ref, k0_ref, k1_ref, k2_ref, k3_ref, kx_ref, v0_ref, v1_ref, v2_ref, v3_ref, vx_ref,
                o_ref, s_ref, *, n_lat_blocks, seq):
    step = pl.program_id(1)
    blk = ATT_BLOCK
    n_lat_keys = 3 * blk
    pairs_per_kv = (A_HEADS // A_KV_HEADS) // 2
    k_blocks = [k0_ref[...], k1_ref[...], k2_ref[...], k3_ref[...]]
    v_blocks = [v0_ref[...], v1_ref[...], v2_ref[...], v3_ref[...]]
    n_keys = n_lat_keys + kx_ref.shape[0]
    lo = lax.broadcasted_iota(jnp.int32, (n_keys, LANES), 1) < A_HEAD_DIM
    zero = jnp.zeros((n_keys, LANES), BF16)
    qo = lax.broadcasted_iota(jnp.int32, (blk, n_keys), 0)
    ko = lax.broadcasted_iota(jnp.int32, (blk, n_keys), 1)
    rel = ko - blk - qo
    in_window = jnp.where(jnp.abs(rel) <= WINDOW, 1, 0)
    is_ctx_key = jnp.where(ko >= n_lat_keys, 1, 0)

    kalls, valls, biases = [], [], []
    for qb in range(WIN_Q_BLOCKS):
        kalls.append(jnp.concatenate(k_blocks[qb:qb + 3] + [kx_ref[...]], axis=0))
        valls.append(jnp.concatenate(v_blocks[qb:qb + 3] + [vx_ref[...]], axis=0))
        i = step * WIN_Q_BLOCKS + qb
        kpos = i * blk - blk + ko
        in_range = jnp.where(kpos >= 0, 1, 0) * jnp.where(kpos < seq, 1, 0)
        latent_query = jnp.where(i < n_lat_blocks, 1, 0)
        valid = (in_window * in_range * latent_query + is_ctx_key) > 0
        bias = jnp.where(valid, 0.0, -1e30).astype(F32)
        biases.append(jnp.concatenate([bias] * pairs_per_kv, axis=0))

    units = [(qb, kv, par) for qb in range(WIN_Q_BLOCKS) for kv in range(A_KV_HEADS) for par in range(2)]

    def lane_half(x, kv, par):
        src = x[:, 0:LANES] if (kv + par) % 2 == 0 else x[:, LANES:2 * LANES]
        return jnp.where(lo, src, zero) if par == 0 else jnp.where(lo, zero, src)

    def scores(u):
        qb, kv, par = units[u]
        qg = q_ref[kv * pairs_per_kv:(kv + 1) * pairs_per_kv, qb * blk:(qb + 1) * blk, :]
        qg = qg.reshape(pairs_per_kv * blk, LANES)
        s_ref[u % 2] = _dot_nt(qg, lane_half(kalls[qb], kv, par))

    def finish(u):
        qb, kv, par = units[u]
        s = s_ref[u % 2] + biases[qb]
        sink_col = LOG2E * jnp.concatenate(
            [jnp.full((blk, 1), sink_ref[kv * 2 * pairs_per_kv + 2 * pp + par], F32)
             for pp in range(pairs_per_kv)], axis=0)
        m = jnp.maximum(jnp.max(s, axis=-1, keepdims=True), sink_col)
        p = jnp.exp2(s - m)
        denom = jnp.sum(p, axis=-1, keepdims=True) + jnp.exp2(sink_col - m)
        return _dot(p.astype(BF16), lane_half(valls[qb], kv, par)) / denom

    scores(0)
    acc = None
    for u, (qb, kv, par) in enumerate(units):
        if u + 1 < len(units):
            scores(u + 1)
        out = finish(u)
        acc = out if par == 0 else acc + out
        if par == 1:
            for pp in range(pairs_per_kv):
                col = (kv * pairs_per_kv + pp) * LANES
                o_ref[qb * blk:(qb + 1) * blk, col:col + LANES] = acc[pp * blk:(pp + 1) * blk].astype(BF16)


def _win_attention(q, kk, vv, sink, batch, seq, ctx_len, cast=None):
    n_pairs, n_rows, _ = q.shape
    blk = ATT_BLOCK
    nq = WIN_Q_BLOCKS
    nlb = seq // blk
    ncb = ctx_len // blk
    assert nq == 2 and nlb % nq == 0 and ncb % nq == 0
    lat_steps = nlb // nq
    ctx_steps = ncb // nq

    def q_idx(b, i):
        return jnp.where(i < lat_steps, b * lat_steps + i, batch * lat_steps + b * ctx_steps + (i - lat_steps))

    def k_idx(b, i, j):
        return b * nlb + jnp.clip(i * nq - 1 + j, 0, nlb - 1)

    ctx_block0 = (batch * seq) // ctx_len
    two = 2 * LANES

    def nb(j):
        return pl.BlockSpec((blk, two), lambda b, i, s: (k_idx(b, i, j), 0))

    ctx_spec = pl.BlockSpec((ctx_len, two), lambda b, i, s: (ctx_block0 + b, 0))
    neighbours = [nb(j) for j in range(nq + 2)]
    steps = lat_steps + ctx_steps
    in_specs = ([pl.BlockSpec((n_pairs, nq * blk, LANES), lambda b, i, s: (0, q_idx(b, i), 0))]
                + neighbours + [ctx_spec] + neighbours + [ctx_spec])
    args = [sink, q] + [kk] * (nq + 3) + [vv] * (nq + 3)
    out_specs = [pl.BlockSpec((nq * blk, A_Q_W), lambda b, i, s: (q_idx(b, i), 0))]
    out_shape = [jax.ShapeDtypeStruct((n_rows, A_Q_W), BF16)]
    kernel_fn = functools.partial(_win_kernel, n_lat_blocks=nlb, seq=seq)
    if cast is not None:
        stacked, which = cast
        c_in, c_out, c_shape = _weight_cast_specs(stacked, which, batch * steps, lambda b, i, s: b * steps + i)
        kernel_fn = _with_weight_cast(kernel_fn, 1 + len(in_specs), 1)
        in_specs += c_in
        args += list(stacked)
        out_specs += c_out
        out_shape += c_shape
    grid_spec = pltpu.PrefetchScalarGridSpec(
        num_scalar_prefetch=1,
        grid=(batch, steps),
        in_specs=in_specs,
        out_specs=out_specs,
        scratch_shapes=[pltpu.VMEM((2, (n_pairs // A_KV_HEADS) * blk, 3 * blk + ctx_len), F32)],
    )
    outs = pl.pallas_call(
        kernel_fn,
        grid_spec=grid_spec,
        out_shape=out_shape,
        compiler_params=_cparams(("arbitrary", "arbitrary")),
        name="win_attn",
    )(*args)
    return outs[0], (tuple(outs[1:]) if cast is not None else None)


def _gmlp_kernel(u_ref, gv_ref, ws_ref, bias_ref, o_ref):
    n_chunks = u_ref.shape[0] // B_CHUNK
    for g in range(B_GROUPS):
        cols = slice(g * LANES, (g + 1) * LANES)
        rhs = jnp.concatenate(
            [gv_ref[c * B_CHUNK:(c + 1) * B_CHUNK, cols] for c in range(n_chunks)], axis=1)
        mixed = _dot(ws_ref[g], rhs) + bias_ref[:, g:g + 1]
        for c in range(n_chunks):
            rows = slice(c * B_CHUNK, (c + 1) * B_CHUNK)
            gu = jax.nn.gelu(u_ref[rows, cols])
            o_ref[rows, cols] = (gu * mixed[:, c * LANES:(c + 1) * LANES]).astype(BF16)


def _gmlp(u, gv, ws, bias_t):
    n_rows = u.shape[0]
    return pl.pallas_call(
        _gmlp_kernel,
        grid=(n_rows // ROW_TILE,),
        in_specs=[
            pl.BlockSpec((ROW_TILE, B_W), lambda i: (i, 0)),
            pl.BlockSpec((ROW_TILE, B_W), lambda i: (i, 0)),
            pl.BlockSpec(ws.shape, lambda i: (0, 0, 0)),
            pl.BlockSpec(bias_t.shape, lambda i: (0, 0)),
        ],
        out_specs=pl.BlockSpec((ROW_TILE, B_W), lambda i: (i, 0)),
        out_shape=jax.ShapeDtypeStruct((n_rows, B_W), BF16),
        compiler_params=_cparams(("arbitrary",)),
        name="gmlp",
    )(u, gv, ws, bias_t)


def _ab_out_kernel(x_ref, oa_ref, ob_ref, w_ref, gt_ref, o_ref):
    half = oa_ref.shape[1]
    y = _dot(oa_ref[...], w_ref[0:half, :]) + _dot(ob_ref[...], w_ref[half:2 * half, :])
    o_ref[...] = x_ref[...] + gt_ref[...] * y


def _ab_out(x, oa, ob, w_out, cidx, gate, cast=None):
    n_rows, d = x.shape
    half = oa.shape[1]
    (y,), wb = _row_tile_call(
        _ab_out_kernel,
        n_rows // ROW_TILE,
        in_specs=[
            pl.BlockSpec((ROW_TILE, d), lambda i: (i, 0)),
            pl.BlockSpec((ROW_TILE, half), lambda i: (i, 0)),
            pl.BlockSpec((ROW_TILE, half), lambda i: (i, 0)),
            _resident(w_out.shape),
            pl.BlockSpec((None, 1, d), lambda i: (cidx(i), 0, 0)),
        ],
        out_specs=[pl.BlockSpec((ROW_TILE, d), lambda i: (i, 0))],
        out_shape=[jax.ShapeDtypeStruct((n_rows, d), F32)],
        name="ab_out",
        args=(x, oa, ob, w_out, gate),
        cast=cast,
    )
    return y, wb


def _cd_in_kernel(x_ref, g_ref, sh_ref, sc_ref, w_ref, qn_ref, kvn_ref, cos_ref, sin_ref,
                  cq_ref, ckv_ref, kr_ref, db_ref, t_ref):
    h = _modulate(x_ref[...], g_ref[...], sh_ref[...], sc_ref[...]).astype(BF16)
    o0 = C_Q_RANK
    o1 = o0 + C_KV_RANK
    o2 = o1 + D_W
    o3 = o2 + D_W
    o4 = o3 + D_W
    cq_ref[...] = (_rms(_dot(h, w_ref[:, 0:o0])) * qn_ref[...]).astype(BF16)
    ckv_ref[...] = (_rms(_dot(h, w_ref[:, o0:o1])) * kvn_ref[...]).astype(BF16)
    db_ref[...] = _dot(h, w_ref[:, o1:o2])
    t_ref[...] = _dot(h, w_ref[:, o2:o3]) * _dot(h, w_ref[:, o3:o4])
    kr_ref[...] = _rope(_dot(h, w_ref[:, o4:o4 + LANES]), cos_ref[...], sin_ref[...]).astype(BF16)


def _cd_in(x, cidx, tab_idx, g, shift, scale, w_in, q_norm, kv_norm, cos_t, sin_t):
    n_rows, d = x.shape
    vec = pl.BlockSpec((None, 1, d), lambda i: (cidx(i), 0, 0))
    tab = pl.BlockSpec((ROW_TILE, LANES), lambda i: (tab_idx(i), 0))

    def rows(width):
        return pl.BlockSpec((ROW_TILE, width), lambda i: (i, 0))

    return pl.pallas_call(
        _cd_in_kernel,
        grid=(n_rows // ROW_TILE,),
        in_specs=[
            rows(d),
            pl.BlockSpec((1, d), lambda i: (0, 0)),
            vec, vec,
            _resident(w_in.shape),
            pl.BlockSpec((1, C_Q_RANK), lambda i: (0, 0)),
            pl.BlockSpec((1, C_KV_RANK), lambda i: (0, 0)),
            tab, tab,
        ],
        out_specs=[rows(C_Q_RANK), rows(C_KV_RANK), rows(LANES), rows(D_W), rows(D_W)],
        out_shape=[
            jax.ShapeDtypeStruct((n_rows, C_Q_RANK), BF16),
            jax.ShapeDtypeStruct((n_rows, C_KV_RANK), BF16),
            jax.ShapeDtypeStruct((n_rows, LANES), BF16),
            jax.ShapeDtypeStruct((n_rows, D_W), F32),
            jax.ShapeDtypeStruct((n_rows, D_W), F32),
        ],
        compiler_params=_cparams(("arbitrary",)),
        name="cd_in",
    )(x, g.reshape(1, d), shift, scale, w_in, q_norm.reshape(1, -1), kv_norm.reshape(1, -1),
      cos_t, sin_t)


def _mla_q_kernel(cq_ref, wn_ref, wr_ref, cos_ref, sin_ref, q_ref):
    cq = cq_ref[...]
    scale = (C_NOPE + C_ROPE) ** -0.5 * LOG2E
    qn = _dot(cq, wn_ref[...]) * scale
    qr = _dot(cq, wr_ref[...]) * scale
    cos = cos_ref[...]
    sin = sin_ref[...]
    lo = lax.broadcasted_iota(jnp.int32, (cq.shape[0], LANES), 1) < C_ROPE
    for pair in range(C_HEADS // 2):
        slab = _rope(qr[:, pair * LANES:(pair + 1) * LANES], cos, sin)
        for par in range(2):
            hd = 2 * pair + par
            q_ref[:, 2 * hd * LANES:(2 * hd + 1) * LANES] = (
                qn[:, hd * LANES:(hd + 1) * LANES].astype(BF16))
            rot = jnp.where(lo, slab, 0.0) if par == 0 else jnp.where(lo, 0.0, slab)
            q_ref[:, (2 * hd + 1) * LANES:(2 * hd + 2) * LANES] = rot.astype(BF16)


def _mla_q(cq, n_rows, w_qn, w_qr, cos_t, sin_t, tab_idx):
    width = 2 * LANES * C_HEADS
    tab = pl.BlockSpec((ROW_TILE, LANES), lambda i: (tab_idx(i), 0))
    return pl.pallas_call(
        _mla_q_kernel,
        grid=(n_rows // ROW_TILE,),
        in_specs=[
            pl.BlockSpec((ROW_TILE, C_Q_RANK), lambda i: (i, 0)),
            _resident(w_qn.shape), _resident(w_qr.shape),
            tab, tab,
        ],
        out_specs=pl.BlockSpec((ROW_TILE, width), lambda i: (i, 0)),
        out_shape=jax.ShapeDtypeStruct((n_rows, width), BF16),
        compiler_params=_cparams(("arbitrary",)),
        name="mla_q",
    )(cq, w_qn, w_qr, cos_t, sin_t)


def _mla_kv_kernel(ckv_ref, kr_ref, wk_ref, wv_ref, k_ref, v_ref):
    ckv = ckv_ref[...]
    kn = _dot(ckv, wk_ref[...])
    kr = kr_ref[...]
    for hd in range(C_HEADS):
        k_ref[:, 2 * hd * LANES:(2 * hd + 1) * LANES] = kn[:, hd * LANES:(hd + 1) * LANES].astype(BF16)
        k_ref[:, (2 * hd + 1) * LANES:(2 * hd + 2) * LANES] = kr
    v_ref[...] = _dot(ckv, wv_ref[...]).astype(BF16)


def _mla_kv(ckv, kr2, w_kn, w_v, batch, seq, ctx_len):
    n_rows = ckv.shape[0]
    tile = ctx_len
    tiles_per_seq = seq // tile
    n_lat_tiles = batch * tiles_per_seq
    kw = 2 * LANES * C_HEADS
    vw = C_V * C_HEADS

    def out_idx(j):
        is_lat = j < n_lat_tiles
        b = jnp.where(is_lat, j // tiles_per_seq, j - n_lat_tiles)
        r = jnp.where(is_lat, j % tiles_per_seq, tiles_per_seq)
        return b, r, 0

    return pl.pallas_call(
        _mla_kv_kernel,
        grid=(n_rows // tile,),
        in_specs=[
            pl.BlockSpec((tile, C_KV_RANK), lambda j: (j, 0)),
            pl.BlockSpec((tile, LANES), lambda j: (j, 0)),
            _resident(w_kn.shape), _resident(w_v.shape),
        ],
        out_specs=[pl.BlockSpec((None, tile, kw), out_idx),
                   pl.BlockSpec((None, tile, vw), out_idx)],
        out_shape=[jax.ShapeDtypeStruct((batch, seq + ctx_len, kw), BF16),
                   jax.ShapeDtypeStruct((batch, seq + ctx_len, vw), BF16)],
        compiler_params=_cparams(("arbitrary",)),
        name="mla_kv",
    )(ckv, kr2, w_kn, w_v)


def _mla_kernel(q_ref, k_ref, v_ref, o_ref, s_ref, m_ref, l_ref, acc_ref, *, tk, n_chunks):
    q = q_ref[...]
    reps = tk // LANES

    def rows(c):
        start = c * tk
        if not isinstance(start, int):
            start = pl.multiple_of(start, tk)
        return pl.ds(start, tk)

    def scores(c, slot):
        s_ref[slot] = _dot_nt(q, k_ref[rows(c), :])

    def absorb(c, slot):
        v = v_ref[rows(c), :]
        for rb in range(q.shape[0] // MLA_ROW_BLOCK):
            blk = slice(rb * MLA_ROW_BLOCK, (rb + 1) * MLA_ROW_BLOCK)
            s = s_ref[slot, blk, :]
            m_old = m_ref[blk, :]
            m_new = jnp.maximum(m_old, jnp.broadcast_to(jnp.max(s, axis=-1, keepdims=True), m_old.shape))
            alpha = jnp.exp2(m_old - m_new)
            p = jnp.exp2(s - jnp.concatenate([m_new] * reps, axis=1))
            l_ref[blk, :] = alpha * l_ref[blk, :] + jnp.broadcast_to(
                jnp.sum(p, axis=-1, keepdims=True), m_old.shape)
            acc_ref[blk, :] = alpha * acc_ref[blk, :] + _dot(p.astype(BF16), v)
            m_ref[blk, :] = m_new

    m_ref[...] = jnp.full(m_ref.shape, -1e30, F32)
    l_ref[...] = jnp.zeros(l_ref.shape, F32)
    acc_ref[...] = jnp.zeros(acc_ref.shape, F32)
    scores(0, 0)
    n_pairs = (n_chunks - 1) // 2

    def pair(i, carry):
        scores(2 * i + 1, 1)
        absorb(2 * i, 0)
        scores(2 * i + 2, 0)
        absorb(2 * i + 1, 1)
        return carry

    for i in range(n_pairs):
        pair(i, 0)
    done = 2 * n_pairs
    if n_chunks - done == 2:
        scores(done + 1, 1)
        absorb(done, 0)
        absorb(done + 1, 1)
    else:
        absorb(done, 0)
    o_ref[...] = (acc_ref[...] / l_ref[...]).astype(BF16)


def _mla_key_tile(n_keys):
    best = LANES
    for t in range(LANES, MLA_K_TILE_MAX + 1, LANES):
        if n_keys % t == 0:
            best = t
    return best


def _mla_attention(q, k, v, batch, seq, cast=None):
    n_lat = batch * seq
    n_keys = k.shape[1]
    tq = MLA_Q_TILE
    nq = seq // tq
    tk = _mla_key_tile(n_keys)
    kw = 2 * LANES
    in_specs = [
        pl.BlockSpec((tq, kw), lambda b, h, i: (b * nq + i, h)),
        pl.BlockSpec((None, n_keys, kw), lambda b, h, i: (b, 0, h)),
        pl.BlockSpec((None, n_keys, C_V), lambda b, h, i: (b, 0, h)),
    ]
    args = [q, k, v]
    out_specs = [pl.BlockSpec((tq, C_V), lambda b, h, i: (b * nq + i, h))]
    out_shape = [jax.ShapeDtypeStruct((n_lat, C_V * C_HEADS), BF16)]
    kernel_fn = functools.partial(_mla_kernel, tk=tk, n_chunks=n_keys // tk)
    if cast is not None:
        stacked, which = cast
        c_in, c_out, c_shape = _weight_cast_specs(stacked, which, batch * C_HEADS * nq,
                                                  lambda b, h, i: (b * C_HEADS + h) * nq + i)
        kernel_fn = _with_weight_cast(kernel_fn, len(in_specs), 1)
        in_specs += c_in
        args += list(stacked)
        out_specs += c_out
        out_shape += c_shape
    outs = pl.pallas_call(
        kernel_fn,
        grid=(batch, C_HEADS, nq),
        in_specs=in_specs,
        out_specs=out_specs,
        out_shape=out_shape,
        scratch_shapes=[pltpu.VMEM((2, tq, tk), F32),
                        pltpu.VMEM((tq, LANES), F32),
                        pltpu.VMEM((tq, LANES), F32),
                        pltpu.VMEM((tq, C_V), F32)],
        compiler_params=_cparams(("arbitrary", "arbitrary", "arbitrary")),
        name="mla_attn",
    )(*args)
    return outs[0], (tuple(outs[1:]) if cast is not None else None)


def _cd_out_kernel(x_ref, oc_ref, db_ref, t_ref, tp_ref, tn_ref, cw_ref, w_ref, gt_ref, o_ref,
                   *, tiles_per_seq):
    i = pl.program_id(0)
    t = t_ref[...]
    rows = t.shape[0]
    row = lax.broadcasted_iota(jnp.int32, t.shape, 0)
    pos = i % tiles_per_seq
    above = jnp.where(pos > 0, tp_ref[SUBLANES - 1:SUBLANES, :], 0.0)
    below = jnp.where(pos < tiles_per_seq - 1, tn_ref[0:1, :], 0.0)
    t_up = jnp.where(row == 0, above, pltpu.roll(t, 1, 0))
    t_dn = jnp.where(row == rows - 1, below, pltpu.roll(t, rows - 1, 0))
    conv = cw_ref[0:1, :] * t_up + cw_ref[1:2, :] * t + cw_ref[2:3, :] * t_dn
    od = (db_ref[...] * conv).astype(BF16)
    half = oc_ref.shape[1]
    y = _dot(oc_ref[...], w_ref[0:half, :]) + _dot(od, w_ref[half:2 * half, :])
    o_ref[...] = x_ref[...] + gt_ref[...] * y


def _cd_out(x, n_rows, oc, db, t, conv_w, w_out, cidx, gate, tiles_per_seq, cast=None):
    d = x.shape[1]
    halo_per_tile = ROW_TILE // SUBLANES
    n_halo = t.shape[0] // SUBLANES
    (y,), wb = _row_tile_call(
        functools.partial(_cd_out_kernel, tiles_per_seq=tiles_per_seq),
        n_rows // ROW_TILE,
        in_specs=[
            pl.BlockSpec((ROW_TILE, d), lambda i: (i, 0)),
            pl.BlockSpec((ROW_TILE, D_W), lambda i: (i, 0)),
            pl.BlockSpec((ROW_TILE, D_W), lambda i: (i, 0)),
            pl.BlockSpec((ROW_TILE, D_W), lambda i: (i, 0)),
            pl.BlockSpec((SUBLANES, D_W), lambda i: (jnp.maximum(i * halo_per_tile - 1, 0), 0)),
            pl.BlockSpec((SUBLANES, D_W),
                         lambda i: (jnp.minimum((i + 1) * halo_per_tile, n_halo - 1), 0)),
            pl.BlockSpec(conv_w.shape, lambda i: (0, 0)),
            _resident(w_out.shape),
            pl.BlockSpec((None, 1, d), lambda i: (cidx(i), 0, 0)),
        ],
        out_specs=[pl.BlockSpec((ROW_TILE, d), lambda i: (i, 0))],
        out_shape=[jax.ShapeDtypeStruct((n_rows, d), F32)],
        name="cd_out",
        args=(x, oc, db, t, t, t, conv_w, w_out, gate),
        cast=cast,
    )
    return y, wb


def _rope_tables(seq, n_ctx_rows):
    t = jnp.arange(seq)
    row = (t // GRID_W).astype(F32)
    col = (t % GRID_W).astype(F32)
    axis_dim = ROPE_DIM // 2
    inv_freq = ROPE_BASE ** (-jnp.arange(0, axis_dim, 2, dtype=F32) / axis_dim)
    ang_r = row[:, None] * inv_freq
    ang_c = col[:, None] * inv_freq
    cos_h = jnp.concatenate([jnp.cos(ang_r)] * 2 + [jnp.cos(ang_c)] * 2, axis=-1)
    sin_h = jnp.concatenate([-jnp.sin(ang_r), jnp.sin(ang_r), -jnp.sin(ang_c), jnp.sin(ang_c)], axis=-1)
    reps = LANES // ROPE_DIM
    cos_t = jnp.concatenate([jnp.tile(cos_h, (1, reps)), jnp.ones((n_ctx_rows, LANES), F32)], axis=0)
    sin_t = jnp.concatenate([jnp.tile(sin_h, (1, reps)), jnp.zeros((n_ctx_rows, LANES), F32)], axis=0)
    return cos_t, sin_t


def kernel(x, c, ctx, c_ctx, mod_w, mod_b, norm_g, ffn_w1, ffn_w3, ffn_w2, ab_w_in, ab_w_out, a_sink,
           b_ws, b_bias, cd_w_in, cd_w_out, c_q_norm, c_kv_norm, c_w_uq, c_w_ukv, d_conv_w, final_norm):
    batch, seq, d = x.shape
    ctx_len = ctx.shape[1]
    depth = mod_w.shape[0]
    n_lat = batch * seq
    n_ctx = batch * ctx_len
    n_all = n_lat + n_ctx
    assert depth == 2, "layer 0 mixes with A||B, layer 1 (last) with C||D"
    assert seq % ROW_TILE == 0 and n_ctx % ROW_TILE == 0 and ROW_TILE % ctx_len == 0
    assert seq % GRID_W == 0 and seq % MLA_Q_TILE == 0 and seq % ctx_len == 0
    assert ctx_len % ATT_BLOCK == 0 and n_lat % ctx_len == 0
    assert (N_MOD * d) % MOD_TILE == 0 and ffn_w1.shape[-1] % FFN_TILE == 0 and d % LANES == 0

    tiles_per_seq = seq // ROW_TILE
    n_lat_tiles = n_lat // ROW_TILE

    def cidx(i):
        return jnp.minimum(i // tiles_per_seq, batch)

    def tab_idx(i):
        return jnp.where(i < n_lat_tiles, i % tiles_per_seq, tiles_per_seq + i - n_lat_tiles)

    cond = jnp.zeros((SUBLANES, d), F32).at[:batch].set(c).at[batch].set(c_ctx)
    mod = _modvec(cond, mod_w, mod_b)
    mod = mod[:, :batch + 1].reshape(depth, batch + 1, N_MOD, 1, d)

    def mvec(layer, k):
        return mod[layer, :, k]

    cos_t, sin_t = _rope_tables(seq, n_ctx)

    wb0 = (ffn_w1[0, 0].astype(BF16), ffn_w3[0, 0].astype(BF16), ffn_w2[0, 0].astype(BF16))

    def ffn_cast(layer, k):
        return (ffn_w1, ffn_w3, ffn_w2), (layer, k)

    xs = _ffn(x.reshape(n_lat, d), n_all, cidx, norm_g[0, 0], mvec(0, 0), mvec(0, 1), mvec(0, 2), *wb0,
              x_ctx=ctx.reshape(n_ctx, d))
    (q, kk, vv, u, gv), wb1 = _ab_in(xs, cidx, tab_idx, norm_g[0, 1], mvec(0, 3), mvec(0, 4),
                                     ab_w_in[0].astype(BF16), cos_t, sin_t, cast=ffn_cast(0, 1))
    oa, wb2 = _win_attention(q, kk, vv, a_sink[0], batch, seq, ctx_len, cast=ffn_cast(1, 0))
    ob = _gmlp(u, gv, b_ws[0].astype(BF16), b_bias[0].T)
    xs, _ = _ab_out(xs, oa, ob, ab_w_out[0].astype(BF16), cidx, mvec(0, 5))
    xs = _ffn(xs, n_all, cidx, norm_g[0, 2], mvec(0, 6), mvec(0, 7), mvec(0, 8), *wb1)

    xs = _ffn(xs, n_all, cidx, norm_g[1, 0], mvec(1, 0), mvec(1, 1), mvec(1, 2), *wb2)
    wi = cd_w_in[0]
    o_kr = C_Q_RANK + C_KV_RANK
    w_kr = wi[:, o_kr:o_kr + C_ROPE]
    w_cd = jnp.concatenate([wi[:, :o_kr], wi[:, o_kr + C_ROPE:], w_kr, w_kr], axis=1).astype(BF16)
    cq, ckv, kr2, db, t = _cd_in(xs, cidx, tab_idx, norm_g[1, 1], mvec(1, 3), mvec(1, 4), w_cd,
                                 c_q_norm[0], c_kv_norm[0], cos_t, sin_t)
    w_uq = c_w_uq[0].reshape(C_Q_RANK, C_HEADS, C_NOPE + C_ROPE)
    w_qn = w_uq[:, :, :C_NOPE].reshape(C_Q_RANK, C_HEADS * C_NOPE).astype(BF16)
    w_qr = w_uq[:, :, C_NOPE:].reshape(C_Q_RANK, C_HEADS * C_ROPE).astype(BF16)
    w_ukv = c_w_ukv[0].reshape(C_KV_RANK, C_HEADS, C_NOPE + C_V)
    w_kn = w_ukv[:, :, :C_NOPE].reshape(C_KV_RANK, C_HEADS * C_NOPE).astype(BF16)
    w_v = w_ukv[:, :, C_NOPE:].reshape(C_KV_RANK, C_HEADS * C_V).astype(BF16)
    qm = _mla_q(cq, n_lat, w_qn, w_qr, cos_t, sin_t, tab_idx)
    km, vm = _mla_kv(ckv, kr2, w_kn, w_v, batch, seq, ctx_len)
    oc, wb3 = _mla_attention(qm, km, vm, batch, seq, cast=ffn_cast(1, 1))
    xl, _ = _cd_out(xs, n_lat, oc, db, t, d_conv_w[0], cd_w_out[0].astype(BF16), cidx, mvec(1, 5), tiles_per_seq)
    out = _ffn(xl, n_lat, cidx, norm_g[1, 2], mvec(1, 6), mvec(1, 7), mvec(1, 8), *wb3,
               final_g=final_norm)
    return out.reshape(batch, seq, d)
```

```python
import functools

import jax
import jax.numpy as jnp
from jax import lax
from jax.experimental import pallas as pl
from jax.experimental.pallas import tpu as pltpu

F32 = jnp.float32
BF16 = jnp.bfloat16

GRID_W = 64
ROPE_BASE = 10000.0
ROPE_DIM = 64
EPS = 1e-6
LOG2E = 1.4426950408889634
N_MOD = 9
WINDOW = 128
A_HEADS = 16
A_KV_HEADS = 2
A_HEAD_DIM = 64
A_Q_W = A_HEADS * A_HEAD_DIM
A_KV_W = A_KV_HEADS * A_HEAD_DIM
B_GROUPS = 8
B_CHUNK = 128
B_W = 1024
C_HEADS = 8
C_NOPE = 128
C_ROPE = ROPE_DIM
C_V = 128
C_Q_RANK = 768
C_KV_RANK = 512
D_W = 1024

LANES = 128
SUBLANES = 8
ROW_TILE = 512
FFN_TILE = 512
ATT_BLOCK = 128
WIN_Q_BLOCKS = 2
MLA_Q_TILE = 1024
MLA_K_TILE_MAX = 768
MLA_ROW_BLOCK = 256
MOD_TILE = 1024
VMEM_LIMIT = 56 * 1024 * 1024


def _cparams(sem):
    return pltpu.CompilerParams(dimension_semantics=sem, vmem_limit_bytes=VMEM_LIMIT)


def _resident(shape):
    nd = len(shape)
    return pl.BlockSpec(shape, lambda *_: (0,) * nd, pipeline_mode=pl.Buffered(1))


def _rms(x):
    return x * lax.rsqrt(jnp.mean(x * x, axis=-1, keepdims=True) + EPS)


def _modulate(x, g, shift, scale):
    return (_rms(x) * g) * (1.0 + scale) + shift


def _rope(z, cos, sin):
    lane = lax.broadcasted_iota(jnp.int32, z.shape, 1)
    first = (lane % 32) < 16
    partner = jnp.where(first, pltpu.roll(z, LANES - 16, 1), pltpu.roll(z, 16, 1))
    return z * cos + partner * sin


def _dot(a, b):
    return jnp.dot(a, b, preferred_element_type=F32)


def _dot_nt(a, b):
    return lax.dot_general(a, b, (((1,), (1,)), ((), ())), preferred_element_type=F32)


def _with_weight_cast(kernel_fn, n_in, n_out):
    def wrapped(*refs):
        core_in = refs[:n_in]
        cast_in = refs[n_in:n_in + 3]
        core_out = refs[n_in + 3:n_in + 3 + n_out]
        cast_out = refs[n_in + 3 + n_out:n_in + 6 + n_out]
        kernel_fn(*core_in, *core_out, *refs[n_in + 6 + n_out:])
        for src, dst in zip(cast_in, cast_out):
            dst[...] = src[...].astype(BF16)
    return wrapped


def _weight_cast_specs(stacked, which, n_steps, lin):
    layer, k = which
    in_specs, out_specs, out_shapes = [], [], []
    for w in stacked:
        rows, cols = w.shape[-2:]
        n_slabs = 1
        while n_slabs * 2 <= n_steps and rows % (n_slabs * 2 * 16) == 0:
            n_slabs *= 2
        repeat = n_steps // n_slabs
        n_blocks = -(-n_steps // repeat)
        slab_rows = rows // n_slabs

        def in_map(*idx, repeat=repeat, n_slabs=n_slabs):
            return layer, k, jnp.minimum(lin(*idx) // repeat, n_slabs - 1), 0

        def out_map(*idx, repeat=repeat):
            return lin(*idx) // repeat, 0

        in_specs.append(pl.BlockSpec((None, None, slab_rows, cols), in_map))
        out_specs.append(pl.BlockSpec((slab_rows, cols), out_map))
        out_shapes.append(jax.ShapeDtypeStruct((n_blocks * slab_rows, cols), BF16))
    return in_specs, out_specs, out_shapes


def _row_tile_call(kernel_fn, n_tiles, in_specs, args, out_specs, out_shape, name, cast=None):
    if cast is None:
        outs = pl.pallas_call(kernel_fn, grid=(n_tiles,), in_specs=in_specs, out_specs=out_specs,
                              out_shape=out_shape, compiler_params=_cparams(("arbitrary",)), name=name)(*args)
        return list(outs), None
    stacked, which = cast
    c_in, c_out, c_shape = _weight_cast_specs(stacked, which, n_tiles, lambda i: i)
    outs = pl.pallas_call(
        _with_weight_cast(kernel_fn, len(in_specs), len(out_specs)),
        grid=(n_tiles,),
        in_specs=list(in_specs) + c_in,
        out_specs=list(out_specs) + c_out,
        out_shape=list(out_shape) + c_shape,
        compiler_params=_cparams(("arbitrary",)),
        name=name,
    )(*args, *stacked)
    n = len(out_specs)
    return list(outs[:n]), tuple(outs[n:])


def _modvec_kernel(c_ref, w_ref, b_ref, o_ref):
    c = c_ref[...]
    s = c * jax.nn.sigmoid(c)
    o_ref[...] = _dot(s.astype(BF16), w_ref[...].astype(BF16)) + b_ref[...]


def _modvec(cond, mod_w, mod_b):
    depth, d, n = mod_w.shape
    rows = cond.shape[0]
    return pl.pallas_call(
        _modvec_kernel,
        grid=(depth, n // MOD_TILE),
        in_specs=[
            pl.BlockSpec((rows, d), lambda l, j: (0, 0)),
            pl.BlockSpec((None, d, MOD_TILE), lambda l, j: (l, 0, j)),
            pl.BlockSpec((None, 1, MOD_TILE), lambda l, j: (l, 0, j)),
        ],
        out_specs=pl.BlockSpec((None, rows, MOD_TILE), lambda l, j: (l, 0, j)),
        out_shape=jax.ShapeDtypeStruct((depth, rows, n), F32),
        compiler_params=_cparams(("arbitrary", "arbitrary")),
        name="modvec",
    )(cond, mod_w, mod_b.reshape(depth, 1, n))


def _ffn_kernel(*refs, n_ffn_tiles, n_pro, n_lat_tiles, final):
    refs = list(refs)
    x0_ref, xn_ref = refs[0:2]
    pos = 2
    c_ref = None
    if n_lat_tiles is not None:
        c_ref = refs[pos]
        pos += 1
    g_ref, sh_ref, sc_ref, shn_ref, scn_ref, gt_ref, w1_ref, w3_ref, w2_ref = refs[pos:pos + 9]
    pos += 9
    fn_ref = None
    if final:
        fn_ref = refs[pos]
        pos += 1
    o_ref, h_ref, res_ref, acc_ref = refs[pos:pos + 4]
    i = pl.program_id(0)
    j = pl.program_id(1)
    cur = i % 2
    g = g_ref[...]

    @pl.when((i == 0) & (j == 0))
    def _():
        x0 = x0_ref[...]
        res_ref[0] = x0
        h_ref[0] = _modulate(x0, g, sh_ref[...], sc_ref[...]).astype(BF16)

    h = h_ref[cur]
    a = _dot(h, w1_ref[...])
    b = _dot(h, w3_ref[...])
    act = ((a * jax.nn.sigmoid(a)) * b).astype(BF16)
    acc_ref[...] = jnp.where(j > 0, acc_ref[...], 0.0) + _dot(act, w2_ref[...])

    pro_rows = xn_ref.shape[0]
    r0 = pl.multiple_of(jnp.minimum(j, n_pro - 1) * pro_rows, pro_rows)
    nxt = xn_ref[...]
    if c_ref is not None:
        nxt = jnp.where(i + 1 < n_lat_tiles, nxt, c_ref[...])
    res_ref[1 - cur, pl.ds(r0, pro_rows), :] = nxt
    h_ref[1 - cur, pl.ds(r0, pro_rows), :] = _modulate(nxt, g, shn_ref[...], scn_ref[...]).astype(BF16)

    @pl.when(j == n_ffn_tiles - 1)
    def _():
        y = res_ref[cur] + (0.5 * gt_ref[...]) * acc_ref[...]
        if final:
            y = _rms(y) * fn_ref[...]
        o_ref[...] = y


def _ffn(x, n_rows, cidx, g, shift, scale, gate, w1, w3, w2, final_g=None, x_ctx=None):
    d = x.shape[1]
    nj = w1.shape[-1] // FFN_TILE
    n_tiles = n_rows // ROW_TILE
    final = final_g is not None
    n_pro = 1
    while n_pro * 2 <= min(nj, ROW_TILE // 16):
        n_pro *= 2
    pro_rows = ROW_TILE // n_pro
    n_lat_tiles = None
    if x_ctx is not None:
        assert x_ctx.shape[0] == ROW_TILE and x.shape[0] + ROW_TILE == n_rows
        n_lat_tiles = n_tiles - 1
    last_src = (n_lat_tiles if x_ctx is not None else n_tiles) - 1

    def nxt(i):
        return jnp.minimum(i + 1, n_tiles - 1)

    def chunk(j):
        return jnp.minimum(j, n_pro - 1)

    vec = pl.BlockSpec((None, 1, d), lambda i, j: (cidx(i), 0, 0))
    vec_next = pl.BlockSpec((None, 1, d), lambda i, j: (cidx(nxt(i)), 0, 0))
    const_row = pl.BlockSpec((1, d), lambda i, j: (0, 0))
    in_specs = [
        pl.BlockSpec((ROW_TILE, d), lambda i, j: (0, 0), pipeline_mode=pl.Buffered(1)),
        pl.BlockSpec((pro_rows, d), lambda i, j: (jnp.minimum(i + 1, last_src) * n_pro + chunk(j), 0)),
    ]
    args = [x, x]
    if x_ctx is not None:
        in_specs.append(pl.BlockSpec((pro_rows, d), lambda i, j: (chunk(j), 0)))
        args.append(x_ctx)
    in_specs += [
        const_row, vec, vec, vec_next, vec_next, vec,
        pl.BlockSpec((d, FFN_TILE), lambda i, j: (0, j)),
        pl.BlockSpec((d, FFN_TILE), lambda i, j: (0, j)),
        pl.BlockSpec((FFN_TILE, d), lambda i, j: (j, 0)),
    ]
    args += [g.reshape(1, d), shift, scale, shift, scale, gate, w1, w3, w2]
    if final:
        in_specs.append(const_row)
        args.append(final_g.reshape(1, d))
    return pl.pallas_call(
        functools.partial(_ffn_kernel, n_ffn_tiles=nj, n_pro=n_pro, n_lat_tiles=n_lat_tiles, final=final),
        grid=(n_tiles, nj),
        in_specs=in_specs,
        out_specs=pl.BlockSpec((ROW_TILE, d), lambda i, j: (i, 0)),
        out_shape=jax.ShapeDtypeStruct((n_rows, d), F32),
        scratch_shapes=[pltpu.VMEM((2, ROW_TILE, d), BF16), pltpu.VMEM((2, ROW_TILE, d), F32),
                        pltpu.VMEM((ROW_TILE, d), F32)],
        compiler_params=_cparams(("arbitrary", "arbitrary")),
        name="ffn_final" if final else ("ffn_split" if x_ctx is not None else "ffn"),
    )(*args)


def _ab_in_kernel(x_ref, g_ref, sh_ref, sc_ref, w_ref, cos_ref, sin_ref,
                  q_ref, kk_ref, vv_ref, u_ref, gv_ref):
    h = _modulate(x_ref[...], g_ref[...], sh_ref[...], sc_ref[...]).astype(BF16)
    cos = cos_ref[...]
    sin = sin_ref[...]
    q_scale = A_HEAD_DIM ** -0.5 * LOG2E
    zq = _dot(h, w_ref[:, 0:A_Q_W])
    for p in range(A_Q_W // LANES):
        slab = _rope(zq[:, p * LANES:(p + 1) * LANES], cos, sin)
        q_ref[p] = (slab * q_scale).astype(BF16)
    zkv = _dot(h, w_ref[:, A_Q_W:A_Q_W + 2 * A_KV_W])
    k = _rope(zkv[:, 0:LANES], cos, sin)
    v = zkv[:, LANES:2 * LANES]
    kk_ref[:, 0:LANES] = k.astype(BF16)
    kk_ref[:, LANES:2 * LANES] = pltpu.roll(k, A_HEAD_DIM, 1).astype(BF16)
    vv_ref[:, 0:LANES] = v.astype(BF16)
    vv_ref[:, LANES:2 * LANES] = pltpu.roll(v, A_HEAD_DIM, 1).astype(BF16)
    off = A_Q_W + 2 * A_KV_W
    u_ref[...] = _dot(h, w_ref[:, off:off + B_W])
    gv_ref[...] = jax.nn.gelu(_dot(h, w_ref[:, off + B_W:off + 2 * B_W])).astype(BF16)


def _ab_in(x, cidx, tab_idx, g, shift, scale, w_in, cos_t, sin_t, cast=None):
    n_rows, d = x.shape
    vec = pl.BlockSpec((None, 1, d), lambda i: (cidx(i), 0, 0))
    tab = pl.BlockSpec((ROW_TILE, LANES), lambda i: (tab_idx(i), 0))
    n_pairs = A_Q_W // LANES
    return _row_tile_call(
        _ab_in_kernel,
        n_rows // ROW_TILE,
        in_specs=[
            pl.BlockSpec((ROW_TILE, d), lambda i: (i, 0)),
            pl.BlockSpec((1, d), lambda i: (0, 0)),
            vec, vec,
            _resident(w_in.shape),
            tab, tab,
        ],
        out_specs=[
            pl.BlockSpec((n_pairs, ROW_TILE, LANES), lambda i: (0, i, 0)),
            pl.BlockSpec((ROW_TILE, 2 * LANES), lambda i: (i, 0)),
            pl.BlockSpec((ROW_TILE, 2 * LANES), lambda i: (i, 0)),
            pl.BlockSpec((ROW_TILE, B_W), lambda i: (i, 0)),
            pl.BlockSpec((ROW_TILE, B_W), lambda i: (i, 0)),
        ],
        out_shape=[
            jax.ShapeDtypeStruct((n_pairs, n_rows, LANES), BF16),
            jax.ShapeDtypeStruct((n_rows, 2 * LANES), BF16),
            jax.ShapeDtypeStruct((n_rows, 2 * LANES), BF16),
            jax.ShapeDtypeStruct((n_rows, B_W), F32),
            jax.ShapeDtypeStruct((n_rows, B_W), BF16),
        ],
        name="ab_in",
        args=(x, g.reshape(1, d), shift, scale, w_in, cos_t, sin_t),
        cast=cast,
    )


def _win_kernel(sink_ref, q_ref, k0_ref, k1_ref, k2_ref, k3_ref, kx_ref, v0_ref, v1_ref, v2_ref, v3_ref, vx_ref,
                o_ref, s_ref, *, n_lat_blocks, seq):
    step = pl.program_id(1)
    blk = ATT_BLOCK
    n_lat_keys = 3 * blk
    pairs_per_kv = (A_HEADS // A_KV_HEADS) // 2
    k_blocks = [k0_ref[...], k1_ref[...], k2_ref[...], k3_ref[...]]
    v_blocks = [v0_ref[...], v1_ref[...], v2_ref[...], v3_ref[...]]
    n_keys = n_lat_keys + kx_ref.shape[0]
    lo = lax.broadcasted_iota(jnp.int32, (n_keys, LANES), 1) < A_HEAD_DIM
    zero = jnp.zeros((n_keys, LANES), BF16)
    qo = lax.broadcasted_iota(jnp.int32, (blk, n_keys), 0)
    ko = lax.broadcasted_iota(jnp.int32, (blk, n_keys), 1)
    rel = ko - blk - qo
    in_window = jnp.where(jnp.abs(rel) <= WINDOW, 1, 0)
    is_ctx_key = jnp.where(ko >= n_lat_keys, 1, 0)

    kalls, valls, biases = [], [], []
    for qb in range(WIN_Q_BLOCKS):
        kalls.append(jnp.concatenate(k_blocks[qb:qb + 3] + [kx_ref[...]], axis=0))
        valls.append(jnp.concatenate(v_blocks[qb:qb + 3] + [vx_ref[...]], axis=0))
        i = step * WIN_Q_BLOCKS + qb
        kpos = i * blk - blk + ko
        in_range = jnp.where(kpos >= 0, 1, 0) * jnp.where(kpos < seq, 1, 0)
        latent_query = jnp.where(i < n_lat_blocks, 1, 0)
        valid = (in_window * in_range * latent_query + is_ctx_key) > 0
        bias = jnp.where(valid, 0.0, -1e30).astype(F32)
        biases.append(jnp.concatenate([bias] * pairs_per_kv, axis=0))

    units = [(qb, kv, par) for qb in range(WIN_Q_BLOCKS) for kv in range(A_KV_HEADS) for par in range(2)]

    def lane_half(x, kv, par):
        src = x[:, 0:LANES] if (kv + par) % 2 == 0 else x[:, LANES:2 * LANES]
        return jnp.where(lo, src, zero) if par == 0 else jnp.where(lo, zero, src)

    def scores(u):
        qb, kv, par = units[u]
        qg = q_ref[kv * pairs_per_kv:(kv + 1) * pairs_per_kv, qb * blk:(qb + 1) * blk, :]
        qg = qg.reshape(pairs_per_kv * blk, LANES)
        s_ref[u % 2] = _dot_nt(qg, lane_half(kalls[qb], kv, par))

    def finish(u):
        qb, kv, par = units[u]
        s = s_ref[u % 2] + biases[qb]
        sink_col = LOG2E * jnp.concatenate(
            [jnp.full((blk, 1), sink_ref[kv * 2 * pairs_per_kv + 2 * pp + par], F32)
             for pp in range(pairs_per_kv)], axis=0)
        m = jnp.maximum(jnp.max(s, axis=-1, keepdims=True), sink_col)
        p = jnp.exp2(s - m)
        denom = jnp.sum(p, axis=-1, keepdims=True) + jnp.exp2(sink_col - m)
        return _dot(p.astype(BF16), lane_half(valls[qb], kv, par)) / denom

    scores(0)
    acc = None
    for u, (qb, kv, par) in enumerate(units):
        if u + 1 < len(units):
            scores(u + 1)
        out = finish(u)
        acc = out if par == 0 else acc + out
        if par == 1:
            for pp in range(pairs_per_kv):
                col = (kv * pairs_per_kv + pp) * LANES
                o_ref[qb * blk:(qb + 1) * blk, col:col + LANES] = acc[pp * blk:(pp + 1) * blk].astype(BF16)


def _win_attention(q, kk, vv, sink, batch, seq, ctx_len, cast=None):
    n_pairs, n_rows, _ = q.shape
    blk = ATT_BLOCK
    nq = WIN_Q_BLOCKS
    nlb = seq // blk
    ncb = ctx_len // blk
    assert nq == 2 and nlb % nq == 0 and ncb % nq == 0
    lat_steps = nlb // nq
    ctx_steps = ncb // nq

    def q_idx(b, i):
        return jnp.where(i < lat_steps, b * lat_steps + i, batch * lat_steps + b * ctx_steps + (i - lat_steps))

    def k_idx(b, i, j):
        return b * nlb + jnp.clip(i * nq - 1 + j, 0, nlb - 1)

    ctx_block0 = (batch * seq) // ctx_len
    two = 2 * LANES

    def nb(j):
        return pl.BlockSpec((blk, two), lambda b, i, s: (k_idx(b, i, j), 0))

    ctx_spec = pl.BlockSpec((ctx_len, two), lambda b, i, s: (ctx_block0 + b, 0))
    neighbours = [nb(j) for j in range(nq + 2)]
    steps = lat_steps + ctx_steps
    in_specs = ([pl.BlockSpec((n_pairs, nq * blk, LANES), lambda b, i, s: (0, q_idx(b, i), 0))]
                + neighbours + [ctx_spec] + neighbours + [ctx_spec])
    args = [sink, q] + [kk] * (nq + 3) + [vv] * (nq + 3)
    out_specs = [pl.BlockSpec((nq * blk, A_Q_W), lambda b, i, s: (q_idx(b, i), 0))]
    out_shape = [jax.ShapeDtypeStruct((n_rows, A_Q_W), BF16)]
    kernel_fn = functools.partial(_win_kernel, n_lat_blocks=nlb, seq=seq)
    if cast is not None:
        stacked, which = cast
        c_in, c_out, c_shape = _weight_cast_specs(stacked, which, batch * steps, lambda b, i, s: b * steps + i)
        kernel_fn = _with_weight_cast(kernel_fn, 1 + len(in_specs), 1)
        in_specs += c_in
        args += list(stacked)
        out_specs += c_out
        out_shape += c_shape
    grid_spec = pltpu.PrefetchScalarGridSpec(
        num_scalar_prefetch=1,
        grid=(batch, steps),
        in_specs=in_specs,
        out_specs=out_specs,
        scratch_shapes=[pltpu.VMEM((2, (n_pairs // A_KV_HEADS) * blk, 3 * blk + ctx_len), F32)],
    )
    outs = pl.pallas_call(
        kernel_fn,
        grid_spec=grid_spec,
        out_shape=out_shape,
        compiler_params=_cparams(("arbitrary", "arbitrary")),
        name="win_attn",
    )(*args)
    return outs[0], (tuple(outs[1:]) if cast is not None else None)


def _gmlp_kernel(u_ref, gv_ref, ws_ref, bias_ref, o_ref):
    n_chunks = u_ref.shape[0] // B_CHUNK
    for g in range(B_GROUPS):
        cols = slice(g * LANES, (g + 1) * LANES)
        rhs = jnp.concatenate(
            [gv_ref[c * B_CHUNK:(c + 1) * B_CHUNK, cols] for c in range(n_chunks)], axis=1)
        mixed = _dot(ws_ref[g], rhs) + bias_ref[:, g:g + 1]
        for c in range(n_chunks):
            rows = slice(c * B_CHUNK, (c + 1) * B_CHUNK)
            gu = jax.nn.gelu(u_ref[rows, cols])
            o_ref[rows, cols] = (gu * mixed[:, c * LANES:(c + 1) * LANES]).astype(BF16)


def _ab_out_kernel(x_ref, oa_ref, u_ref, gv_ref, ws_ref, bias_ref, w_ref, gt_ref, o_ref, ob_ref):
    _gmlp_kernel(u_ref, gv_ref, ws_ref, bias_ref, ob_ref)
    half = oa_ref.shape[1]
    y = _dot(oa_ref[...], w_ref[0:half, :]) + _dot(ob_ref[...], w_ref[half:2 * half, :])
    o_ref[...] = x_ref[...] + gt_ref[...] * y


def _ab_out(x, oa, u, gv, ws, bias_t, w_out, cidx, gate):
    n_rows, d = x.shape
    half = oa.shape[1]
    return pl.pallas_call(
        _ab_out_kernel,
        grid=(n_rows // ROW_TILE,),
        in_specs=[
            pl.BlockSpec((ROW_TILE, d), lambda i: (i, 0)),
            pl.BlockSpec((ROW_TILE, half), lambda i: (i, 0)),
            pl.BlockSpec((ROW_TILE, B_W), lambda i: (i, 0)),
            pl.BlockSpec((ROW_TILE, B_W), lambda i: (i, 0)),
            pl.BlockSpec(ws.shape, lambda i: (0, 0, 0)),
            pl.BlockSpec(bias_t.shape, lambda i: (0, 0)),
            _resident(w_out.shape),
            pl.BlockSpec((None, 1, d), lambda i: (cidx(i), 0, 0)),
        ],
        out_specs=pl.BlockSpec((ROW_TILE, d), lambda i: (i, 0)),
        out_shape=jax.ShapeDtypeStruct((n_rows, d), F32),
        scratch_shapes=[pltpu.VMEM((ROW_TILE, B_W), BF16)],
        compiler_params=_cparams(("arbitrary",)),
        name="ab_out",
    )(x, oa, u, gv, ws, bias_t, w_out, gate)


def _cd_in_kernel(x_ref, g_ref, sh_ref, sc_ref, w_ref, qn_ref, kvn_ref, cos_ref, sin_ref,
                  cq_ref, ckv_ref, kr_ref, db_ref, t_ref):
    h = _modulate(x_ref[...], g_ref[...], sh_ref[...], sc_ref[...]).astype(BF16)
    o0 = C_Q_RANK
    o1 = o0 + C_KV_RANK
    o2 = o1 + D_W
    o3 = o2 + D_W
    o4 = o3 + D_W
    cq_ref[...] = (_rms(_dot(h, w_ref[:, 0:o0])) * qn_ref[...]).astype(BF16)
    ckv_ref[...] = (_rms(_dot(h, w_ref[:, o0:o1])) * kvn_ref[...]).astype(BF16)
    db_ref[...] = _dot(h, w_ref[:, o1:o2])
    t_ref[...] = _dot(h, w_ref[:, o2:o3]) * _dot(h, w_ref[:, o3:o4])
    kr_ref[...] = _rope(_dot(h, w_ref[:, o4:o4 + LANES]), cos_ref[...], sin_ref[...]).astype(BF16)


def _cd_in(x, cidx, tab_idx, g, shift, scale, w_in, q_norm, kv_norm, cos_t, sin_t):
    n_rows, d = x.shape
    vec = pl.BlockSpec((None, 1, d), lambda i: (cidx(i), 0, 0))
    tab = pl.BlockSpec((ROW_TILE, LANES), lambda i: (tab_idx(i), 0))

    def rows(width):
        return pl.BlockSpec((ROW_TILE, width), lambda i: (i, 0))

    return pl.pallas_call(
        _cd_in_kernel,
        grid=(n_rows // ROW_TILE,),
        in_specs=[
            rows(d),
            pl.BlockSpec((1, d), lambda i: (0, 0)),
            vec, vec,
            _resident(w_in.shape),
            pl.BlockSpec((1, C_Q_RANK), lambda i: (0, 0)),
            pl.BlockSpec((1, C_KV_RANK), lambda i: (0, 0)),
            tab, tab,
        ],
        out_specs=[rows(C_Q_RANK), rows(C_KV_RANK), rows(LANES), rows(D_W), rows(D_W)],
        out_shape=[
            jax.ShapeDtypeStruct((n_rows, C_Q_RANK), BF16),
            jax.ShapeDtypeStruct((n_rows, C_KV_RANK), BF16),
            jax.ShapeDtypeStruct((n_rows, LANES), BF16),
            jax.ShapeDtypeStruct((n_rows, D_W), F32),
            jax.ShapeDtypeStruct((n_rows, D_W), F32),
        ],
        compiler_params=_cparams(("arbitrary",)),
        name="cd_in",
    )(x, g.reshape(1, d), shift, scale, w_in, q_norm.reshape(1, -1), kv_norm.reshape(1, -1),
      cos_t, sin_t)


def _mla_q_kernel(cq_ref, wn_ref, wr_ref, cos_ref, sin_ref, q_ref):
    cq = cq_ref[...]
    scale = (C_NOPE + C_ROPE) ** -0.5 * LOG2E
    qn = _dot(cq, wn_ref[...]) * scale
    qr = _dot(cq, wr_ref[...]) * scale
    cos = cos_ref[...]
    sin = sin_ref[...]
    lo = lax.broadcasted_iota(jnp.int32, (cq.shape[0], LANES), 1) < C_ROPE
    for pair in range(C_HEADS // 2):
        slab = _rope(qr[:, pair * LANES:(pair + 1) * LANES], cos, sin)
        for par in range(2):
            hd = 2 * pair + par
            q_ref[:, 2 * hd * LANES:(2 * hd + 1) * LANES] = (
                qn[:, hd * LANES:(hd + 1) * LANES].astype(BF16))
            rot = jnp.where(lo, slab, 0.0) if par == 0 else jnp.where(lo, 0.0, slab)
            q_ref[:, (2 * hd + 1) * LANES:(2 * hd + 2) * LANES] = rot.astype(BF16)


def _mla_q(cq, n_rows, w_qn, w_qr, cos_t, sin_t, tab_idx):
    width = 2 * LANES * C_HEADS
    tab = pl.BlockSpec((ROW_TILE, LANES), lambda i: (tab_idx(i), 0))
    return pl.pallas_call(
        _mla_q_kernel,
        grid=(n_rows // ROW_TILE,),
        in_specs=[
            pl.BlockSpec((ROW_TILE, C_Q_RANK), lambda i: (i, 0)),
            _resident(w_qn.shape), _resident(w_qr.shape),
            tab, tab,
        ],
        out_specs=pl.BlockSpec((ROW_TILE, width), lambda i: (i, 0)),
        out_shape=jax.ShapeDtypeStruct((n_rows, width), BF16),
        compiler_params=_cparams(("arbitrary",)),
        name="mla_q",
    )(cq, w_qn, w_qr, cos_t, sin_t)


def _mla_kv_kernel(ckv_ref, kr_ref, wk_ref, wv_ref, k_ref, v_ref):
    ckv = ckv_ref[...]
    kn = _dot(ckv, wk_ref[...])
    kr = kr_ref[...]
    for hd in range(C_HEADS):
        k_ref[:, 2 * hd * LANES:(2 * hd + 1) * LANES] = kn[:, hd * LANES:(hd + 1) * LANES].astype(BF16)
        k_ref[:, (2 * hd + 1) * LANES:(2 * hd + 2) * LANES] = kr
    v_ref[...] = _dot(ckv, wv_ref[...]).astype(BF16)


def _mla_kv(ckv, kr2, w_kn, w_v, batch, seq, ctx_len):
    n_rows = ckv.shape[0]
    tile = ctx_len
    tiles_per_seq = seq // tile
    n_lat_tiles = batch * tiles_per_seq
    kw = 2 * LANES * C_HEADS
    vw = C_V * C_HEADS

    def out_idx(j):
        is_lat = j < n_lat_tiles
        b = jnp.where(is_lat, j // tiles_per_seq, j - n_lat_tiles)
        r = jnp.where(is_lat, j % tiles_per_seq, tiles_per_seq)
        return b, r, 0

    return pl.pallas_call(
        _mla_kv_kernel,
        grid=(n_rows // tile,),
        in_specs=[
            pl.BlockSpec((tile, C_KV_RANK), lambda j: (j, 0)),
            pl.BlockSpec((tile, LANES), lambda j: (j, 0)),
            _resident(w_kn.shape), _resident(w_v.shape),
        ],
        out_specs=[pl.BlockSpec((None, tile, kw), out_idx),
                   pl.BlockSpec((None, tile, vw), out_idx)],
        out_shape=[jax.ShapeDtypeStruct((batch, seq + ctx_len, kw), BF16),
                   jax.ShapeDtypeStruct((batch, seq + ctx_len, vw), BF16)],
        compiler_params=_cparams(("arbitrary",)),
        name="mla_kv",
    )(ckv, kr2, w_kn, w_v)


def _mla_kernel(q_ref, k_ref, v_ref, o_ref, s_ref, m_ref, l_ref, acc_ref, *, tk, n_chunks):
    q = q_ref[...]
    reps = tk // LANES

    def rows(c):
        start = c * tk
        if not isinstance(start, int):
            start = pl.multiple_of(start, tk)
        return pl.ds(start, tk)

    def scores(c, slot):
        s_ref[slot] = _dot_nt(q, k_ref[rows(c), :])

    def absorb(c, slot):
        v = v_ref[rows(c), :]
        for rb in range(q.shape[0] // MLA_ROW_BLOCK):
            blk = slice(rb * MLA_ROW_BLOCK, (rb + 1) * MLA_ROW_BLOCK)
            s = s_ref[slot, blk, :]
            m_old = m_ref[blk, :]
            m_new = jnp.maximum(m_old, jnp.broadcast_to(jnp.max(s, axis=-1, keepdims=True), m_old.shape))
            alpha = jnp.exp2(m_old - m_new)
            p = jnp.exp2(s - jnp.concatenate([m_new] * reps, axis=1))
            l_ref[blk, :] = alpha * l_ref[blk, :] + jnp.broadcast_to(
                jnp.sum(p, axis=-1, keepdims=True), m_old.shape)
            acc_ref[blk, :] = alpha * acc_ref[blk, :] + _dot(p.astype(BF16), v)
            m_ref[blk, :] = m_new

    m_ref[...] = jnp.full(m_ref.shape, -1e30, F32)
    l_ref[...] = jnp.zeros(l_ref.shape, F32)
    acc_ref[...] = jnp.zeros(acc_ref.shape, F32)
    scores(0, 0)
    n_pairs = (n_chunks - 1) // 2

    def pair(i, carry):
        scores(2 * i + 1, 1)
        absorb(2 * i, 0)
        scores(2 * i + 2, 0)
        absorb(2 * i + 1, 1)
        return carry

    for i in range(n_pairs):
        pair(i, 0)
    done = 2 * n_pairs
    if n_chunks - done == 2:
        scores(done + 1, 1)
        absorb(done, 0)
        absorb(done + 1, 1)
    else:
        absorb(done, 0)
    o_ref[...] = (acc_ref[...] / l_ref[...]).astype(BF16)


def _mla_key_tile(n_keys):
    best = LANES
    for t in range(LANES, MLA_K_TILE_MAX + 1, LANES):
        if n_keys % t == 0:
            best = t
    return best


def _mla_attention(q, k, v, batch, seq, cast=None):
    n_lat = batch * seq
    n_keys = k.shape[1]
    tq = MLA_Q_TILE
    nq = seq // tq
    tk = _mla_key_tile(n_keys)
    kw = 2 * LANES
    in_specs = [
        pl.BlockSpec((tq, kw), lambda b, h, i: (b * nq + i, h)),
        pl.BlockSpec((None, n_keys, kw), lambda b, h, i: (b, 0, h)),
        pl.BlockSpec((None, n_keys, C_V), lambda b, h, i: (b, 0, h)),
    ]
    args = [q, k, v]
    out_specs = [pl.BlockSpec((tq, C_V), lambda b, h, i: (b * nq + i, h))]
    out_shape = [jax.ShapeDtypeStruct((n_lat, C_V * C_HEADS), BF16)]
    kernel_fn = functools.partial(_mla_kernel, tk=tk, n_chunks=n_keys // tk)
    if cast is not None:
        stacked, which = cast
        c_in, c_out, c_shape = _weight_cast_specs(stacked, which, batch * C_HEADS * nq,
                                                  lambda b, h, i: (b * C_HEADS + h) * nq + i)
        kernel_fn = _with_weight_cast(kernel_fn, len(in_specs), 1)
        in_specs += c_in
        args += list(stacked)
        out_specs += c_out
        out_shape += c_shape
    outs = pl.pallas_call(
        kernel_fn,
        grid=(batch, C_HEADS, nq),
        in_specs=in_specs,
        out_specs=out_specs,
        out_shape=out_shape,
        scratch_shapes=[pltpu.VMEM((2, tq, tk), F32),
                        pltpu.VMEM((tq, LANES), F32),
                        pltpu.VMEM((tq, LANES), F32),
                        pltpu.VMEM((tq, C_V), F32)],
        compiler_params=_cparams(("arbitrary", "arbitrary", "arbitrary")),
        name="mla_attn",
    )(*args)
    return outs[0], (tuple(outs[1:]) if cast is not None else None)


def _cd_out_kernel(x_ref, oc_ref, db_ref, t_ref, tp_ref, tn_ref, cw_ref, w_ref, gt_ref, o_ref,
                   *, tiles_per_seq):
    i = pl.program_id(0)
    t = t_ref[...]
    rows = t.shape[0]
    row = lax.broadcasted_iota(jnp.int32, t.shape, 0)
    pos = i % tiles_per_seq
    above = jnp.where(pos > 0, tp_ref[SUBLANES - 1:SUBLANES, :], 0.0)
    below = jnp.where(pos < tiles_per_seq - 1, tn_ref[0:1, :], 0.0)
    t_up = jnp.where(row == 0, above, pltpu.roll(t, 1, 0))
    t_dn = jnp.where(row == rows - 1, below, pltpu.roll(t, rows - 1, 0))
    conv = cw_ref[0:1, :] * t_up + cw_ref[1:2, :] * t + cw_ref[2:3, :] * t_dn
    od = (db_ref[...] * conv).astype(BF16)
    half = oc_ref.shape[1]
    y = _dot(oc_ref[...], w_ref[0:half, :]) + _dot(od, w_ref[half:2 * half, :])
    o_ref[...] = x_ref[...] + gt_ref[...] * y


def _cd_out(x, n_rows, oc, db, t, conv_w, w_out, cidx, gate, tiles_per_seq, cast=None):
    d = x.shape[1]
    halo_per_tile = ROW_TILE // SUBLANES
    n_halo = t.shape[0] // SUBLANES
    (y,), wb = _row_tile_call(
        functools.partial(_cd_out_kernel, tiles_per_seq=tiles_per_seq),
        n_rows // ROW_TILE,
        in_specs=[
            pl.BlockSpec((ROW_TILE, d), lambda i: (i, 0)),
            pl.BlockSpec((ROW_TILE, D_W), lambda i: (i, 0)),
            pl.BlockSpec((ROW_TILE, D_W), lambda i: (i, 0)),
            pl.BlockSpec((ROW_TILE, D_W), lambda i: (i, 0)),
            pl.BlockSpec((SUBLANES, D_W), lambda i: (jnp.maximum(i * halo_per_tile - 1, 0), 0)),
            pl.BlockSpec((SUBLANES, D_W),
                         lambda i: (jnp.minimum((i + 1) * halo_per_tile, n_halo - 1), 0)),
            pl.BlockSpec(conv_w.shape, lambda i: (0, 0)),
            _resident(w_out.shape),
            pl.BlockSpec((None, 1, d), lambda i: (cidx(i), 0, 0)),
        ],
        out_specs=[pl.BlockSpec((ROW_TILE, d), lambda i: (i, 0))],
        out_shape=[jax.ShapeDtypeStruct((n_rows, d), F32)],
        name="cd_out",
        args=(x, oc, db, t, t, t, conv_w, w_out, gate),
        cast=cast,
    )
    return y, wb


def _rope_tables(seq, n_ctx_rows):
    t = jnp.arange(seq)
    row = (t // GRID_W).astype(F32)
    col = (t % GRID_W).astype(F32)
    axis_dim = ROPE_DIM // 2
    inv_freq = ROPE_BASE ** (-jnp.arange(0, axis_dim, 2, dtype=F32) / axis_dim)
    ang_r = row[:, None] * inv_freq
    ang_c = col[:, None] * inv_freq
    cos_h = jnp.concatenate([jnp.cos(ang_r)] * 2 + [jnp.cos(ang_c)] * 2, axis=-1)
    sin_h = jnp.concatenate([-jnp.sin(ang_r), jnp.sin(ang_r), -jnp.sin(ang_c), jnp.sin(ang_c)], axis=-1)
    reps = LANES // ROPE_DIM
    cos_t = jnp.concatenate([jnp.tile(cos_h, (1, reps)), jnp.ones((n_ctx_rows, LANES), F32)], axis=0)
    sin_t = jnp.concatenate([jnp.tile(sin_h, (1, reps)), jnp.zeros((n_ctx_rows, LANES), F32)], axis=0)
    return cos_t, sin_t


def kernel(x, c, ctx, c_ctx, mod_w, mod_b, norm_g, ffn_w1, ffn_w3, ffn_w2, ab_w_in, ab_w_out, a_sink,
           b_ws, b_bias, cd_w_in, cd_w_out, c_q_norm, c_kv_norm, c_w_uq, c_w_ukv, d_conv_w, final_norm):
    batch, seq, d = x.shape
    ctx_len = ctx.shape[1]
    depth = mod_w.shape[0]
    n_lat = batch * seq
    n_ctx = batch * ctx_len
    n_all = n_lat + n_ctx
    assert depth == 2, "layer 0 mixes with A||B, layer 1 (last) with C||D"
    assert seq % ROW_TILE == 0 and n_ctx % ROW_TILE == 0 and ROW_TILE % ctx_len == 0
    assert seq % GRID_W == 0 and seq % MLA_Q_TILE == 0 and seq % ctx_len == 0
    assert ctx_len % ATT_BLOCK == 0 and n_lat % ctx_len == 0
    assert (N_MOD * d) % MOD_TILE == 0 and ffn_w1.shape[-1] % FFN_TILE == 0 and d % LANES == 0

    tiles_per_seq = seq // ROW_TILE
    n_lat_tiles = n_lat // ROW_TILE

    def cidx(i):
        return jnp.minimum(i // tiles_per_seq, batch)

    def tab_idx(i):
        return jnp.where(i < n_lat_tiles, i % tiles_per_seq, tiles_per_seq + i - n_lat_tiles)

    cond = jnp.zeros((SUBLANES, d), F32).at[:batch].set(c).at[batch].set(c_ctx)
    mod = _modvec(cond, mod_w, mod_b)
    mod = mod[:, :batch + 1].reshape(depth, batch + 1, N_MOD, 1, d)

    def mvec(layer, k):
        return mod[layer, :, k]

    cos_t, sin_t = _rope_tables(seq, n_ctx)

    wb0 = (ffn_w1[0, 0].astype(BF16), ffn_w3[0, 0].astype(BF16), ffn_w2[0, 0].astype(BF16))

    def ffn_cast(layer, k):
        return (ffn_w1, ffn_w3, ffn_w2), (layer, k)

    xs = _ffn(x.reshape(n_lat, d), n_all, cidx, norm_g[0, 0], mvec(0, 0), mvec(0, 1), mvec(0, 2), *wb0,
              x_ctx=ctx.reshape(n_ctx, d))
    (q, kk, vv, u, gv), wb1 = _ab_in(xs, cidx, tab_idx, norm_g[0, 1], mvec(0, 3), mvec(0, 4),
                                     ab_w_in[0].astype(BF16), cos_t, sin_t, cast=ffn_cast(0, 1))
    oa, wb2 = _win_attention(q, kk, vv, a_sink[0], batch, seq, ctx_len, cast=ffn_cast(1, 0))
    xs = _ab_out(xs, oa, u, gv, b_ws[0].astype(BF16), b_bias[0].T, ab_w_out[0].astype(BF16), cidx, mvec(0, 5))
    xs = _ffn(xs, n_all, cidx, norm_g[0, 2], mvec(0, 6), mvec(0, 7), mvec(0, 8), *wb1)

    xs = _ffn(xs, n_all, cidx, norm_g[1, 0], mvec(1, 0), mvec(1, 1), mvec(1, 2), *wb2)
    wi = cd_w_in[0]
    o_kr = C_Q_RANK + C_KV_RANK
    w_kr = wi[:, o_kr:o_kr + C_ROPE]
    w_cd = jnp.concatenate([wi[:, :o_kr], wi[:, o_kr + C_ROPE:], w_kr, w_kr], axis=1).astype(BF16)
    cq, ckv, kr2, db, t = _cd_in(xs, cidx, tab_idx, norm_g[1, 1], mvec(1, 3), mvec(1, 4), w_cd,
                                 c_q_norm[0], c_kv_norm[0], cos_t, sin_t)
    w_uq = c_w_uq[0].reshape(C_Q_RANK, C_HEADS, C_NOPE + C_ROPE)
    w_qn = w_uq[:, :, :C_NOPE].reshape(C_Q_RANK, C_HEADS * C_NOPE).astype(BF16)
    w_qr = w_uq[:, :, C_NOPE:].reshape(C_Q_RANK, C_HEADS * C_ROPE).astype(BF16)
    w_ukv = c_w_ukv[0].reshape(C_KV_RANK, C_HEADS, C_NOPE + C_V)
    w_kn = w_ukv[:, :, :C_NOPE].reshape(C_KV_RANK, C_HEADS * C_NOPE).astype(BF16)
    w_v = w_ukv[:, :, C_NOPE:].reshape(C_KV_RANK, C_HEADS * C_V).astype(BF16)
    qm = _mla_q(cq, n_lat, w_qn, w_qr, cos_t, sin_t, tab_idx)
    km, vm = _mla_kv(ckv, kr2, w_kn, w_v, batch, seq, ctx_len)
    oc, wb3 = _mla_attention(qm, km, vm, batch, seq, cast=ffn_cast(1, 1))
    xl, _ = _cd_out(xs, n_lat, oc, db, t, d_conv_w[0], cd_w_out[0].astype(BF16), cidx, mvec(1, 5), tiles_per_seq)
    out = _ffn(xl, n_lat, cidx, norm_g[1, 2], mvec(1, 6), mvec(1, 7), mvec(1, 8), *wb3,
               final_g=final_norm)
    return out.reshape(batch, seq, d)
```
